```python
import jax, jax.numpy as jnp
from jax import lax
import numpy as np

D_MODEL = 1024
BATCH = 4
SEQ = 8192
DEPTH = 2

N_MIXERS = 2
D_FF = 4 * D_MODEL
EPS = 1e-6

GLA_HEADS = 4
GLA_DK = D_MODEL // 2
GLA_DV = D_MODEL
GLA_HEAD_K = GLA_DK // GLA_HEADS
GLA_HEAD_V = GLA_DV // GLA_HEADS
GLA_GATE_RANK = 16
GLA_GATE_NORM = 16.0
GLA_CHUNK = 64
GLA_IN = 2 * GLA_DK + 2 * GLA_DV + GLA_GATE_RANK

FOX_HEADS = 16
FOX_HEAD_DIM = D_MODEL // FOX_HEADS
FOX_BLOCK = 128
FOX_IN = 3 * D_MODEL + FOX_HEADS + D_MODEL

kernel_name = "gla_fox_interleaved_hybrid"


def rms_norm(x, g):
    xf = x.astype(jnp.float32)
    y = xf * lax.rsqrt(jnp.mean(xf * xf, axis=-1, keepdims=True) + EPS)
    return (y * g.astype(jnp.float32)).astype(x.dtype)


def sqrelu_mlp(h, w1, w2):
    return jnp.square(jax.nn.relu(h @ w1)) @ w2


def gla_mixer(h, w_in, w_gk_up, b_gk, g_onorm, w_out):
    B, S, _ = h.shape
    C = GLA_CHUNK
    N = S // C
    proj = h @ w_in
    q, k, v, r, gk_low = jnp.split(
        proj, [GLA_DK, 2 * GLA_DK, 2 * GLA_DK + GLA_DV, 2 * GLA_DK + 2 * GLA_DV], axis=-1)
    log_a = jax.nn.log_sigmoid((gk_low @ w_gk_up + b_gk).astype(jnp.float32)) / GLA_GATE_NORM

    def to_chunks(t, dh):
        return t.astype(jnp.float32).reshape(B, N, C, GLA_HEADS, dh).transpose(1, 0, 3, 2, 4)

    qc = to_chunks(q, GLA_HEAD_K) * (GLA_HEAD_K ** -0.5)
    kc = to_chunks(k, GLA_HEAD_K)
    vc = to_chunks(v, GLA_HEAD_V)
    gc = to_chunks(log_a, GLA_HEAD_K)
    causal = jnp.tril(jnp.ones((C, C), dtype=bool))

    def step(state, inp):
        qb, kb, vb, gb = inp
        b = jnp.cumsum(gb, axis=-2)
        o_inter = jnp.einsum('bhck,bhkv->bhcv', qb * jnp.exp(b), state)
        diff = b[:, :, :, None, :] - b[:, :, None, :, :]
        decay = jnp.exp(jnp.where(causal[:, :, None], diff, -jnp.inf))
        attn = jnp.einsum('bhik,bhjk,bhijk->bhij', qb, kb, decay)
        o_intra = jnp.einsum('bhij,bhjv->bhiv', attn, vb)
        b_last = b[:, :, -1:, :]
        k_dec = kb * jnp.exp(b_last - b)
        state = state * jnp.exp(b_last[:, :, 0, :])[..., None] + jnp.einsum('bhck,bhcv->bhkv', k_dec, vb)
        return state, o_inter + o_intra

    state0 = jnp.zeros((B, GLA_HEADS, GLA_HEAD_K, GLA_HEAD_V), jnp.float32)
    _, o = lax.scan(step, state0, (qc, kc, vc, gc))
    o = o.transpose(1, 0, 3, 2, 4).reshape(B, S, GLA_HEADS, GLA_HEAD_V).astype(h.dtype)
    o = rms_norm(o, g_onorm).reshape(B, S, GLA_DV)
    return (o * jax.nn.silu(r)) @ w_out


def fox_mixer(h, w_in, b_f, g_q, g_k, w_out):
    B, S, _ = h.shape
    H, Dh = FOX_HEADS, FOX_HEAD_DIM
    nb = S // FOX_BLOCK
    proj = h @ w_in
    q, k, v, f_logit, o_gate = jnp.split(
        proj, [D_MODEL, 2 * D_MODEL, 3 * D_MODEL, 3 * D_MODEL + H], axis=-1)
    q = rms_norm(q.reshape(B, S, H, Dh), g_q).transpose(0, 2, 1, 3)
    k = rms_norm(k.reshape(B, S, H, Dh), g_k).transpose(0, 2, 1, 3)
    v = v.reshape(B, S, H, Dh).transpose(0, 2, 1, 3)
    log_f = jax.nn.log_sigmoid((f_logit + b_f).astype(jnp.float32))
    c = jnp.cumsum(log_f, axis=1).transpose(0, 2, 1)
    scale = Dh ** -0.5
    key_pos = jnp.arange(S)
    q_blocks = q.reshape(B, H, nb, FOX_BLOCK, Dh).transpose(2, 0, 1, 3, 4)
    c_blocks = c.reshape(B, H, nb, FOX_BLOCK).transpose(2, 0, 1, 3)

    def attend(args):
        qb, cq, blk = args
        s = jnp.einsum('bhqd,bhkd->bhqk', qb, k).astype(jnp.float32) * scale
        s = s + cq[..., None] - c[:, :, None, :]
        q_pos = blk * FOX_BLOCK + jnp.arange(FOX_BLOCK)
        s = jnp.where(key_pos[None, :] <= q_pos[:, None], s, -jnp.inf)
        p = jax.nn.softmax(s, axis=-1)
        return jnp.einsum('bhqk,bhkd->bhqd', p.astype(v.dtype), v)

    o = lax.map(attend, (q_blocks, c_blocks, jnp.arange(nb)))
    o = o.transpose(1, 0, 3, 2, 4).reshape(B, S, D_MODEL)
    return (o * jax.nn.sigmoid(o_gate)) @ w_out


def setup_inputs(seed: int = 0) -> dict:
    key = jax.random.key(seed)
    ks = jax.random.split(key, 24)
    f32 = jnp.float32
    nrm = lambda k, shape, fan_in: jax.random.normal(k, shape, f32) * (fan_in ** -0.5)
    gain = lambda k, n: 1.0 + 0.05 * jax.random.normal(k, (n,), f32)
    return {
        "x": jax.random.normal(ks[0], (BATCH, SEQ, D_MODEL), f32),
        "l0_norm_mix": gain(ks[1], D_MODEL),
        "l0_w_in": nrm(ks[2], (D_MODEL, GLA_IN), D_MODEL),
        "l0_w_gk_up": nrm(ks[3], (GLA_GATE_RANK, GLA_DK), GLA_GATE_RANK),
        "l0_b_gk": 0.1 * jax.random.normal(ks[4], (GLA_DK,), f32),
        "l0_g_onorm": gain(ks[5], GLA_HEAD_V),
        "l0_w_out": nrm(ks[6], (GLA_DV, D_MODEL), GLA_DV),
        "l0_norm_ffn": gain(ks[7], D_MODEL),
        "l0_w_ff1": nrm(ks[8], (D_MODEL, D_FF), D_MODEL),
        "l0_w_ff2": nrm(ks[9], (D_FF, D_MODEL), D_FF),
        "l1_norm_mix": gain(ks[10], D_MODEL),
        "l1_w_in": nrm(ks[11], (D_MODEL, FOX_IN), D_MODEL),
        "l1_b_f": jax.random.uniform(ks[12], (FOX_HEADS,), f32, 1.0, 5.0),
        "l1_g_q": gain(ks[13], FOX_HEAD_DIM),
        "l1_g_k": gain(ks[14], FOX_HEAD_DIM),
        "l1_w_out": nrm(ks[15], (D_MODEL, D_MODEL), D_MODEL),
        "l1_norm_ffn": gain(ks[16], D_MODEL),
        "l1_w_ff1": nrm(ks[17], (D_MODEL, D_FF), D_MODEL),
        "l1_w_ff2": nrm(ks[18], (D_FF, D_MODEL), D_FF),
        "final_norm": gain(ks[19], D_MODEL),
    }


def reference(x, l0_norm_mix, l0_w_in, l0_w_gk_up, l0_b_gk, l0_g_onorm, l0_w_out,
              l0_norm_ffn, l0_w_ff1, l0_w_ff2,
              l1_norm_mix, l1_w_in, l1_b_f, l1_g_q, l1_g_k, l1_w_out,
              l1_norm_ffn, l1_w_ff1, l1_w_ff2, final_norm):
    layers = (
        (l0_norm_mix, (l0_w_in, l0_w_gk_up, l0_b_gk, l0_g_onorm, l0_w_out), l0_norm_ffn, l0_w_ff1, l0_w_ff2),
        (l1_norm_mix, (l1_w_in, l1_b_f, l1_g_q, l1_g_k, l1_w_out), l1_norm_ffn, l1_w_ff1, l1_w_ff2),
    )
    mixers = (gla_mixer, fox_mixer)
    for i in range(DEPTH):
        norm_mix, mix_params, norm_ffn, w1, w2 = layers[i]
        x = x + mixers[i % N_MIXERS](rms_norm(x, norm_mix), *mix_params)
        x = x + sqrelu_mlp(rms_norm(x, norm_ffn), w1, w2)
    return rms_norm(x, final_norm)
```

```python
import functools

import numpy as np
import jax
import jax.numpy as jnp
from jax import lax
from jax.experimental import pallas as pl
from jax.experimental.pallas import tpu as pltpu

F32 = jnp.float32
BF16 = jnp.bfloat16

EPS = 1e-6
D_MODEL = 1024
D_FF = 4 * D_MODEL

GLA_HEADS = 4
GLA_DK = 512
GLA_DV = 1024
GLA_HEAD_K = 128
GLA_HEAD_V = 256
GLA_GATE_RANK = 16
GLA_GATE_NORM = 16.0
GLA_BLOCK = 256
GLA_LEVELS = 8

FOX_HEADS = 16
FOX_HEAD_DIM = 64
FOX_PAIRS = FOX_HEADS // 2

LANES = 128
VMEM_LIMIT = 56 * 1024 * 1024


def _cparams(sem):
    return pltpu.CompilerParams(dimension_semantics=sem, vmem_limit_bytes=VMEM_LIMIT)


def _dot(a, b):
    return jnp.dot(a, b, preferred_element_type=F32)


def _dot_nt(a, b):
    return lax.dot_general(a, b, (((1,), (1,)), ((), ())), preferred_element_type=F32)


def _dot_tn(a, b):
    return lax.dot_general(a, b, (((0,), (0,)), ((), ())), preferred_element_type=F32)


def _rms(x, g):
    ms = jnp.mean(x * x, axis=-1, keepdims=True)
    return x * lax.rsqrt(ms + EPS) * g


def _log_sigmoid(z):
    return jnp.minimum(z, 0.0) - jnp.log1p(jnp.exp(-jnp.abs(z)))


def _sigmoid(z):
    return 1.0 / (1.0 + jnp.exp(-z))


def _norm_mm_kernel(x_ref, g_ref, w_ref, o_ref, xn_ref):
    @pl.when(pl.program_id(1) == 0)
    def _():
        xn_ref[...] = _rms(x_ref[...], g_ref[...]).astype(BF16)

    o_ref[...] = _dot(xn_ref[...], w_ref[...])


def _norm_matmul(x, g, w, tm, tn):
    t, d = x.shape
    n = w.shape[1]
    return pl.pallas_call(
        _norm_mm_kernel,
        grid=(t // tm, n // tn),
        in_specs=[
            pl.BlockSpec((tm, d), lambda i, j: (i, 0)),
            pl.BlockSpec((1, d), lambda i, j: (0, 0)),
            pl.BlockSpec((d, tn), lambda i, j: (0, j)),
        ],
        out_specs=pl.BlockSpec((tm, tn), lambda i, j: (i, j)),
        out_shape=jax.ShapeDtypeStruct((t, n), F32),
        scratch_shapes=[pltpu.VMEM((tm, d), BF16)],
        compiler_params=_cparams(("parallel", "arbitrary")),
    )(x, g.reshape(1, d), w)


def _mm_res_kernel(a_ref, w_ref, x_ref, o_ref):
    o_ref[...] = x_ref[...] + _dot(a_ref[...], w_ref[...])


def _matmul_residual(a, w, x, tm):
    t, d = x.shape
    k = a.shape[1]
    return pl.pallas_call(
        _mm_res_kernel,
        grid=(t // tm,),
        in_specs=[
            pl.BlockSpec((tm, k), lambda i: (i, 0)),
            pl.BlockSpec((k, d), lambda i: (0, 0)),
            pl.BlockSpec((tm, d), lambda i: (i, 0)),
        ],
        out_specs=pl.BlockSpec((tm, d), lambda i: (i, 0)),
        out_shape=jax.ShapeDtypeStruct((t, d), F32),
        compiler_params=_cparams(("parallel",)),
    )(a, w, x)


def _mlp_kernel(x_ref, g_ref, w1_ref, w2_ref, gf_ref, o_ref, xn_ref, acc_ref, *, final_norm):
    j = pl.program_id(1)

    @pl.when(j == 0)
    def _():
        xn_ref[...] = _rms(x_ref[...], g_ref[...]).astype(BF16)
        acc_ref[...] = jnp.zeros_like(acc_ref)

    h = _dot(xn_ref[...], w1_ref[...])
    h = jnp.square(jnp.maximum(h, 0.0)).astype(BF16)
    acc_ref[...] += _dot(h, w2_ref[...])

    @pl.when(j == pl.num_programs(1) - 1)
    def _():
        y = x_ref[...] + acc_ref[...]
        if final_norm:
            y = _rms(y, gf_ref[...])
        o_ref[...] = y


def _mlp(x, g, w1, w2, gf, tm, tf, final_norm):
    t, d = x.shape
    ff = w1.shape[1]
    return pl.pallas_call(
        functools.partial(_mlp_kernel, final_norm=final_norm),
        grid=(t // tm, ff // tf),
        in_specs=[
            pl.BlockSpec((tm, d), lambda i, j: (i, 0)),
            pl.BlockSpec((1, d), lambda i, j: (0, 0)),
            pl.BlockSpec((d, tf), lambda i, j: (0, j)),
            pl.BlockSpec((tf, d), lambda i, j: (j, 0)),
            pl.BlockSpec((1, d), lambda i, j: (0, 0)),
        ],
        out_specs=pl.BlockSpec((tm, d), lambda i, j: (i, 0)),
        out_shape=jax.ShapeDtypeStruct((t, d), F32),
        scratch_shapes=[pltpu.VMEM((tm, d), BF16), pltpu.VMEM((tm, d), F32)],
        compiler_params=_cparams(("parallel", "arbitrary")),
    )(x, g.reshape(1, d), w1, w2, gf.reshape(1, d))


def _gla_level_map(n):
    i = np.arange(n)[:, None]
    j = np.arange(n)[None, :]
    x = i ^ j
    lvl = np.zeros((n, n), np.int32)
    nz = x > 0
    lvl[nz] = np.floor(np.log2(x[nz])).astype(np.int32) + 1
    lvl = np.where(i > j, lvl, 0)
    lvl = np.where(i == j, GLA_LEVELS + 1, lvl)
    return lvl.astype(np.int32)


def _gla_kernel(q_ref, k_ref, v_ref, r_ref, gl_ref, wup_ref, bgk_ref, gon_ref, lvl_ref,
                o_ref, st_ref):
    L = GLA_BLOCK

    @pl.when(pl.program_id(2) == 0)
    def _():
        st_ref[...] = jnp.zeros_like(st_ref)

    z = _dot(gl_ref[...], wup_ref[...]) + bgk_ref[...]
    g = _log_sigmoid(z) * (1.0 / GLA_GATE_NORM)

    qs = q_ref[...] * (GLA_HEAD_K ** -0.5)
    k = k_ref[...]
    lvl = lvl_ref[...]
    row = lax.broadcasted_iota(jnp.int32, (L, GLA_HEAD_K), 0)

    a = jnp.where(lvl == GLA_LEVELS + 1, _dot_nt(qs.astype(BF16), k.astype(BF16)), 0.0)
    p = g
    tot = g
    for t in range(GLA_LEVELS):
        h = 1 << t
        upper = (row & h) != 0
        e = jnp.where(upper, p, tot - p)
        xs = (jnp.where(upper, qs, k) * jnp.exp(e)).astype(BF16)
        a = jnp.where(lvl == t + 1, _dot_nt(xs, xs), a)
        t_lo = pltpu.roll(tot, h, 0)
        t_hi = pltpu.roll(tot, L - h, 0)
        p = p + jnp.where(upper, t_lo, 0.0)
        tot = tot + jnp.where(upper, t_lo, t_hi)

    st = st_ref[...]
    v = v_ref[...].astype(BF16)
    qe = (qs * jnp.exp(p)).astype(BF16)
    o = _dot_nt(qe, st.astype(BF16)) + _dot(a.astype(BF16), v)
    kd = (k * jnp.exp(tot - p)).astype(BF16)
    st_ref[...] = st * jnp.exp(tot[0:1, :]) + _dot_tn(v, kd)

    r = r_ref[...]
    o_ref[...] = (_rms(o, gon_ref[...]) * (r * _sigmoid(r))).astype(o_ref.dtype)


def _gla(proj, gl, wup, bgk, gon, batch, seq):
    L = GLA_BLOCK
    nblk = seq // L
    t = batch * seq
    lvl = jnp.asarray(_gla_level_map(L))
    kb = GLA_DK // GLA_HEAD_K
    vb = 2 * GLA_DK // GLA_HEAD_V
    rb = vb + GLA_HEADS
    rowblk = lambda b, h, s: b * nblk + s
    return pl.pallas_call(
        _gla_kernel,
        grid=(batch, GLA_HEADS, nblk),
        in_specs=[
            pl.BlockSpec((L, GLA_HEAD_K), lambda b, h, s: (rowblk(b, h, s), h)),
            pl.BlockSpec((L, GLA_HEAD_K), lambda b, h, s: (rowblk(b, h, s), kb + h)),
            pl.BlockSpec((L, GLA_HEAD_V), lambda b, h, s: (rowblk(b, h, s), vb + h)),
            pl.BlockSpec((L, GLA_HEAD_V), lambda b, h, s: (rowblk(b, h, s), rb + h)),
            pl.BlockSpec((L, LANES), lambda b, h, s: (rowblk(b, h, s), 0)),
            pl.BlockSpec((LANES, GLA_HEAD_K), lambda b, h, s: (0, h)),
            pl.BlockSpec((1, GLA_HEAD_K), lambda b, h, s: (0, h)),
            pl.BlockSpec((1, GLA_HEAD_V), lambda b, h, s: (0, 0)),
            pl.BlockSpec((L, L), lambda b, h, s: (0, 0)),
        ],
        out_specs=pl.BlockSpec((L, GLA_HEAD_V), lambda b, h, s: (rowblk(b, h, s), h)),
        out_shape=jax.ShapeDtypeStruct((t, GLA_DV), BF16),
        scratch_shapes=[pltpu.VMEM((GLA_HEAD_V, GLA_HEAD_K), F32)],
        compiler_params=_cparams(("parallel", "parallel", "arbitrary")),
    )(proj, proj, proj, proj, gl, wup, bgk.reshape(1, GLA_DK), gon.reshape(1, GLA_HEAD_V), lvl)


def _split3(x):
    hi = x.astype(BF16)
    r = x - hi.astype(F32)
    mid = r.astype(BF16)
    lo = (r - mid.astype(F32)).astype(BF16)
    return hi, mid, lo


def _fox_prep_kernel(q_ref, k_ref, v_ref, fl_ref, bf_ref, gq_ref, gk_ref, bd_ref, tri_ref,
                     qn_ref, kn_ref, vb_ref, c_ref, carry_ref):
    @pl.when(pl.program_id(1) == 0)
    def _():
        carry_ref[...] = jnp.zeros_like(carry_ref)

    bd = bd_ref[...]

    def head_norm(x, gain):
        sq = x * x
        hi, mid, lo = _split3(sq)
        ss = _dot(hi, bd) + _dot(mid, bd) + _dot(lo, bd)
        return x * lax.rsqrt(ss * (1.0 / FOX_HEAD_DIM) + EPS) * gain

    for pblk in range(D_MODEL // LANES):
        sl = slice(pblk * LANES, (pblk + 1) * LANES)
        qn_ref[:, sl] = (head_norm(q_ref[:, sl], gq_ref[:, sl]) * (FOX_HEAD_DIM ** -0.5)).astype(BF16)
        kn_ref[:, sl] = head_norm(k_ref[:, sl], gk_ref[:, sl]).astype(BF16)
    vb_ref[...] = v_ref[...].astype(BF16)

    lf = _log_sigmoid(fl_ref[...] + bf_ref[...])
    hi, mid, lo = _split3(lf)
    tri = tri_ref[...]
    c = carry_ref[...] + (_dot(tri, hi) + _dot(tri, mid) + _dot(tri, lo))
    c_ref[...] = c
    carry_ref[...] = c[c.shape[0] - 1:, :]


def _fox_prep(proj, fl, b_f, g_q, g_k, batch, seq, ts):
    t = batch * seq
    nblk = seq // ts
    d = D_MODEL
    bd = jnp.asarray(np.kron(np.eye(LANES // FOX_HEAD_DIM), np.ones((FOX_HEAD_DIM, FOX_HEAD_DIM))), BF16)
    tri = jnp.asarray(np.tril(np.ones((ts, ts))), BF16)
    bf = jnp.zeros((1, LANES), F32).at[0, :FOX_HEADS].set(b_f)
    gq = jnp.tile(g_q, FOX_HEADS).reshape(1, d)
    gk = jnp.tile(g_k, FOX_HEADS).reshape(1, d)
    rowblk = lambda b, s: b * nblk + s
    const = lambda b, s: (0, 0)
    return pl.pallas_call(
        _fox_prep_kernel,
        grid=(batch, nblk),
        in_specs=[
            pl.BlockSpec((ts, d), lambda b, s: (rowblk(b, s), 0)),
            pl.BlockSpec((ts, d), lambda b, s: (rowblk(b, s), 1)),
            pl.BlockSpec((ts, d), lambda b, s: (rowblk(b, s), 2)),
            pl.BlockSpec((ts, LANES), lambda b, s: (rowblk(b, s), 0)),
            pl.BlockSpec((1, LANES), const),
            pl.BlockSpec((1, d), const),
            pl.BlockSpec((1, d), const),
            pl.BlockSpec((LANES, LANES), const),
            pl.BlockSpec((ts, ts), const),
        ],
        out_specs=[
            pl.BlockSpec((ts, d), lambda b, s: (rowblk(b, s), 0)),
            pl.BlockSpec((ts, d), lambda b, s: (rowblk(b, s), 0)),
            pl.BlockSpec((ts, d), lambda b, s: (rowblk(b, s), 0)),
            pl.BlockSpec((ts, LANES), lambda b, s: (rowblk(b, s), 0)),
        ],
        out_shape=[
            jax.ShapeDtypeStruct((t, d), BF16),
            jax.ShapeDtypeStruct((t, d), BF16),
            jax.ShapeDtypeStruct((t, d), BF16),
            jax.ShapeDtypeStruct((t, LANES), F32),
        ],
        scratch_shapes=[pltpu.VMEM((1, LANES), F32)],
        compiler_params=_cparams(("parallel", "arbitrary")),
    )(proj, proj, proj, fl, bf, gq, gk, bd, tri)


def _fox_attn_kernel(q_ref, k_ref, v_ref, c_ref, gate_ref, o_ref, m_ref, acc_ref, *, tq):
    qi = pl.program_id(2)
    lane = lax.broadcasted_iota(jnp.int32, (1, LANES), 1)
    halves = (lane < FOX_HEAD_DIM, lane >= FOX_HEAD_DIM)
    q2 = q_ref[...]
    qm = [jnp.where(halves[e], q2, jnp.zeros_like(q2)) for e in range(2)]

    m_ref[...] = jnp.full(m_ref.shape, -1e30, F32)
    acc_ref[...] = jnp.zeros_like(acc_ref)

    def block(kj, diagonal):
        start = pl.multiple_of(kj * tq, tq)
        k2 = k_ref[pl.ds(start, tq), :]
        v2 = v_ref[pl.ds(start, tq), :]
        ck = c_ref[0, 0, :, pl.ds(start, tq)]
        if diagonal:
            rr = lax.broadcasted_iota(jnp.int32, (tq, tq), 0)
            cc = lax.broadcasted_iota(jnp.int32, (tq, tq), 1)
            causal = cc <= rr
        for e in range(2):
            s = _dot_nt(qm[e], k2) - ck[e:e + 1, :]
            if diagonal:
                s = jnp.where(causal, s, -jnp.inf)
            m_prev = m_ref[e]
            m_new = jnp.maximum(m_prev, jnp.max(s, axis=1, keepdims=True))
            alpha = jnp.exp(m_prev - m_new)
            p = jnp.exp(s - m_new).astype(BF16)
            vop = jnp.where(halves[e], v2, jnp.ones_like(v2))
            acc_ref[e] = acc_ref[e] * alpha + _dot(p, vop)
            m_ref[e] = m_new

    def body(kj, carry):
        block(kj, False)
        return carry

    lax.fori_loop(0, qi, body, 0)
    block(qi, True)

    outs = []
    for e in range(2):
        acc = acc_ref[e]
        denom = pltpu.roll(acc, FOX_HEAD_DIM, 1)
        outs.append(acc / denom)
    o2 = jnp.where(halves[0], outs[0], outs[1])
    o_ref[...] = (o2 * _sigmoid(gate_ref[...])).astype(o_ref.dtype)


def _fox_attn(qn, kn, vb, c_t, proj, batch, seq, tq):
    t = batch * seq
    nq = seq // tq
    gate_blk = 3 * D_MODEL // LANES
    return pl.pallas_call(
        functools.partial(_fox_attn_kernel, tq=tq),
        grid=(batch, FOX_PAIRS, nq),
        in_specs=[
            pl.BlockSpec((tq, LANES), lambda b, hp, i: (b * nq + i, hp)),
            pl.BlockSpec((seq, LANES), lambda b, hp, i: (b, hp)),
            pl.BlockSpec((seq, LANES), lambda b, hp, i: (b, hp)),
            pl.BlockSpec((1, 1, 2, seq), lambda b, hp, i: (b, hp, 0, 0)),
            pl.BlockSpec((tq, LANES), lambda b, hp, i: (b * nq + i, gate_blk + hp)),
        ],
        out_specs=pl.BlockSpec((tq, LANES), lambda b, hp, i: (b * nq + i, hp)),
        out_shape=jax.ShapeDtypeStruct((t, D_MODEL), BF16),
        scratch_shapes=[pltpu.VMEM((2, tq, 1), F32), pltpu.VMEM((2, tq, LANES), F32)],
        compiler_params=_cparams(("parallel", "parallel", "arbitrary")),
    )(qn, kn, vb, c_t, proj)


def kernel(x, l0_norm_mix, l0_w_in, l0_w_gk_up, l0_b_gk, l0_g_onorm, l0_w_out, l0_norm_ffn, l0_w_ff1, l0_w_ff2, l1_norm_mix, l1_w_in, l1_b_f, l1_g_q, l1_g_k, l1_w_out, l1_norm_ffn, l1_w_ff1, l1_w_ff2, final_norm):
    batch, seq, d = x.shape
    t = batch * seq
    xf = x.reshape(t, d)

    n_main = 2 * GLA_DK + 2 * GLA_DV
    w_main = l0_w_in[:, :n_main].astype(BF16)
    w_gate = jnp.zeros((d, LANES), F32).at[:, :GLA_GATE_RANK].set(l0_w_in[:, n_main:]).astype(BF16)
    w_up = jnp.zeros((LANES, GLA_DK), F32).at[:GLA_GATE_RANK].set(l0_w_gk_up)
    proj = _norm_matmul(xf, l0_norm_mix, w_main, 1024, 512)
    gl = _norm_matmul(xf, l0_norm_mix, w_gate, 1024, LANES)
    og = _gla(proj, gl, w_up, l0_b_gk, l0_g_onorm, batch, seq)
    xf = _matmul_residual(og, l0_w_out.astype(BF16), xf, 1024)
    xf = _mlp(xf, l0_norm_ffn, l0_w_ff1.astype(BF16), l0_w_ff2.astype(BF16), final_norm, 1024, 512, False)

    w_qkv = l1_w_in[:, :3 * d]
    w_f = l1_w_in[:, 3 * d:3 * d + FOX_HEADS]
    w_g = l1_w_in[:, 3 * d + FOX_HEADS:]
    w_main = jnp.concatenate([w_qkv, w_g], axis=1).astype(BF16)
    w_fp = jnp.zeros((d, LANES), F32).at[:, :FOX_HEADS].set(w_f).astype(BF16)
    proj = _norm_matmul(xf, l1_norm_mix, w_main, 1024, 512)
    fl = _norm_matmul(xf, l1_norm_mix, w_fp, 1024, LANES)
    qn, kn, vb, c = _fox_prep(proj, fl, l1_b_f, l1_g_q, l1_g_k, batch, seq, 512)
    c_t = c.reshape(batch, seq, LANES)[:, :, :FOX_HEADS].transpose(0, 2, 1).reshape(batch, FOX_PAIRS, 2, seq)
    oa = _fox_attn(qn, kn, vb, c_t, proj, batch, seq, 512)
    xf = _matmul_residual(oa, l1_w_out.astype(BF16), xf, 1024)
    xf = _mlp(xf, l1_norm_ffn, l1_w_ff1.astype(BF16), l1_w_ff2.astype(BF16), final_norm, 1024, 512, True)
    return xf.reshape(batch, seq, d)
```

```python
import functools

import numpy as np
import jax
import jax.numpy as jnp
from jax import lax
from jax.experimental import pallas as pl
from jax.experimental.pallas import tpu as pltpu

F32 = jnp.float32
BF16 = jnp.bfloat16

EPS = 1e-6
D_MODEL = 1024
D_FF = 4 * D_MODEL

GLA_HEADS = 4
GLA_DK = 512
GLA_DV = 1024
GLA_HEAD_K = 128
GLA_HEAD_V = 256
GLA_GATE_RANK = 16
GLA_GATE_NORM = 16.0
GLA_BLOCK = 256
GLA_LEVELS = 8

FOX_HEADS = 16
FOX_HEAD_DIM = 64
FOX_PAIRS = FOX_HEADS // 2

LANES = 128
VMEM_LIMIT = 56 * 1024 * 1024


def _cparams(sem):
    return pltpu.CompilerParams(dimension_semantics=sem, vmem_limit_bytes=VMEM_LIMIT)


def _dot(a, b):
    return jnp.dot(a, b, preferred_element_type=F32)


def _dot_nt(a, b):
    return lax.dot_general(a, b, (((1,), (1,)), ((), ())), preferred_element_type=F32)


def _dot_tn(a, b):
    return lax.dot_general(a, b, (((0,), (0,)), ((), ())), preferred_element_type=F32)


def _rms(x, g):
    ms = jnp.mean(x * x, axis=-1, keepdims=True)
    return x * lax.rsqrt(ms + EPS) * g


def _log_sigmoid(z):
    return jnp.minimum(z, 0.0) - jnp.log1p(jnp.exp(-jnp.abs(z)))


def _sigmoid(z):
    return 1.0 / (1.0 + jnp.exp(-z))


def _norm_mm_kernel(x_ref, g_ref, w_ref, o_ref, xn_ref):
    @pl.when(pl.program_id(1) == 0)
    def _():
        xn_ref[...] = _rms(x_ref[...], g_ref[...]).astype(BF16)

    o_ref[...] = _dot(xn_ref[...], w_ref[...])


def _norm_matmul(x, g, w, tm, tn):
    t, d = x.shape
    n = w.shape[1]
    return pl.pallas_call(
        _norm_mm_kernel,
        name="norm_matmul",
        grid=(t // tm, n // tn),
        in_specs=[
            pl.BlockSpec((tm, d), lambda i, j: (i, 0)),
            pl.BlockSpec((1, d), lambda i, j: (0, 0)),
            pl.BlockSpec((d, tn), lambda i, j: (0, j)),
        ],
        out_specs=pl.BlockSpec((tm, tn), lambda i, j: (i, j)),
        out_shape=jax.ShapeDtypeStruct((t, n), F32),
        scratch_shapes=[pltpu.VMEM((tm, d), BF16)],
        compiler_params=_cparams(("parallel", "arbitrary")),
    )(x, g.reshape(1, d), w)


def _mm_res_kernel(a_ref, w_ref, x_ref, o_ref):
    o_ref[...] = x_ref[...] + _dot(a_ref[...], w_ref[...])


def _matmul_residual(a, w, x, tm):
    t, d = x.shape
    k = a.shape[1]
    return pl.pallas_call(
        _mm_res_kernel,
        name="matmul_residual",
        grid=(t // tm,),
        in_specs=[
            pl.BlockSpec((tm, k), lambda i: (i, 0)),
            pl.BlockSpec((k, d), lambda i: (0, 0)),
            pl.BlockSpec((tm, d), lambda i: (i, 0)),
        ],
        out_specs=pl.BlockSpec((tm, d), lambda i: (i, 0)),
        out_shape=jax.ShapeDtypeStruct((t, d), F32),
        compiler_params=_cparams(("parallel",)),
    )(a, w, x)


def _mlp_kernel(x_ref, g_ref, w1_ref, w2_ref, gf_ref, o_ref, xn_ref, acc_ref, *, final_norm):
    j = pl.program_id(1)

    @pl.when(j == 0)
    def _():
        xn_ref[...] = _rms(x_ref[...], g_ref[...]).astype(BF16)
        acc_ref[...] = jnp.zeros_like(acc_ref)

    h = _dot(xn_ref[...], w1_ref[...])
    h = jnp.square(jnp.maximum(h, 0.0)).astype(BF16)
    acc_ref[...] += _dot(h, w2_ref[...])

    @pl.when(j == pl.num_programs(1) - 1)
    def _():
        y = x_ref[...] + acc_ref[...]
        if final_norm:
            y = _rms(y, gf_ref[...])
        o_ref[...] = y


def _mlp(x, g, w1, w2, gf, tm, tf, final_norm):
    t, d = x.shape
    ff = w1.shape[1]
    return pl.pallas_call(
        functools.partial(_mlp_kernel, final_norm=final_norm),
        name="mlp",
        grid=(t // tm, ff // tf),
        in_specs=[
            pl.BlockSpec((tm, d), lambda i, j: (i, 0)),
            pl.BlockSpec((1, d), lambda i, j: (0, 0)),
            pl.BlockSpec((d, tf), lambda i, j: (0, j)),
            pl.BlockSpec((tf, d), lambda i, j: (j, 0)),
            pl.BlockSpec((1, d), lambda i, j: (0, 0)),
        ],
        out_specs=pl.BlockSpec((tm, d), lambda i, j: (i, 0)),
        out_shape=jax.ShapeDtypeStruct((t, d), F32),
        scratch_shapes=[pltpu.VMEM((tm, d), BF16), pltpu.VMEM((tm, d), F32)],
        compiler_params=_cparams(("parallel", "arbitrary")),
    )(x, g.reshape(1, d), w1, w2, gf.reshape(1, d))


def _gla_level_map(n):
    i = np.arange(n)[:, None]
    j = np.arange(n)[None, :]
    x = i ^ j
    lvl = np.zeros((n, n), np.int32)
    nz = x > 0
    lvl[nz] = np.floor(np.log2(x[nz])).astype(np.int32) + 1
    lvl = np.where(i > j, lvl, 0)
    lvl = np.where(i == j, GLA_LEVELS + 1, lvl)
    return lvl.astype(np.int32)


def _gla_kernel(q_ref, k_ref, v_ref, r_ref, gl_ref, wup_ref, bgk_ref, gon_ref, lvl_ref,
                o_ref, st_ref):
    L = GLA_BLOCK

    @pl.when(pl.program_id(2) == 0)
    def _():
        st_ref[...] = jnp.zeros_like(st_ref)

    z = _dot(gl_ref[...], wup_ref[...]) + bgk_ref[...]
    g = _log_sigmoid(z) * (1.0 / GLA_GATE_NORM)

    qs = q_ref[...] * (GLA_HEAD_K ** -0.5)
    k = k_ref[...]
    lvl = lvl_ref[...]
    row = lax.broadcasted_iota(jnp.int32, (L, GLA_HEAD_K), 0)

    a = jnp.where(lvl == GLA_LEVELS + 1, _dot_nt(qs.astype(BF16), k.astype(BF16)), 0.0)
    p = g
    tot = g
    for t in range(GLA_LEVELS):
        h = 1 << t
        upper = (row & h) != 0
        e = jnp.where(upper, p, tot - p)
        xs = (jnp.where(upper, qs, k) * jnp.exp(e)).astype(BF16)
        a = jnp.where(lvl == t + 1, _dot_nt(xs, xs), a)
        t_lo = pltpu.roll(tot, h, 0)
        t_hi = pltpu.roll(tot, L - h, 0)
        p = p + jnp.where(upper, t_lo, 0.0)
        tot = tot + jnp.where(upper, t_lo, t_hi)

    st = st_ref[...]
    v = v_ref[...].astype(BF16)
    qe = (qs * jnp.exp(p)).astype(BF16)
    o = _dot_nt(qe, st.astype(BF16)) + _dot(a.astype(BF16), v)
    kd = (k * jnp.exp(tot - p)).astype(BF16)
    st_ref[...] = st * jnp.exp(tot[0:1, :]) + _dot_tn(v, kd)

    r = r_ref[...]
    o_ref[...] = (_rms(o, gon_ref[...]) * (r * _sigmoid(r))).astype(o_ref.dtype)


def _gla(proj, gl, wup, bgk, gon, batch, seq):
    L = GLA_BLOCK
    nblk = seq // L
    t = batch * seq
    lvl = jnp.asarray(_gla_level_map(L))
    kb = GLA_DK // GLA_HEAD_K
    vb = 2 * GLA_DK // GLA_HEAD_V
    rb = vb + GLA_HEADS
    rowblk = lambda b, h, s: b * nblk + s
    return pl.pallas_call(
        _gla_kernel,
        name="gla",
        grid=(batch, GLA_HEADS, nblk),
        in_specs=[
            pl.BlockSpec((L, GLA_HEAD_K), lambda b, h, s: (rowblk(b, h, s), h)),
            pl.BlockSpec((L, GLA_HEAD_K), lambda b, h, s: (rowblk(b, h, s), kb + h)),
            pl.BlockSpec((L, GLA_HEAD_V), lambda b, h, s: (rowblk(b, h, s), vb + h)),
            pl.BlockSpec((L, GLA_HEAD_V), lambda b, h, s: (rowblk(b, h, s), rb + h)),
            pl.BlockSpec((L, LANES), lambda b, h, s: (rowblk(b, h, s), 0)),
            pl.BlockSpec((LANES, GLA_HEAD_K), lambda b, h, s: (0, h)),
            pl.BlockSpec((1, GLA_HEAD_K), lambda b, h, s: (0, h)),
            pl.BlockSpec((1, GLA_HEAD_V), lambda b, h, s: (0, 0)),
            pl.BlockSpec((L, L), lambda b, h, s: (0, 0)),
        ],
        out_specs=pl.BlockSpec((L, GLA_HEAD_V), lambda b, h, s: (rowblk(b, h, s), h)),
        out_shape=jax.ShapeDtypeStruct((t, GLA_DV), BF16),
        scratch_shapes=[pltpu.VMEM((GLA_HEAD_V, GLA_HEAD_K), F32)],
        compiler_params=_cparams(("parallel", "parallel", "arbitrary")),
    )(proj, proj, proj, proj, gl, wup, bgk.reshape(1, GLA_DK), gon.reshape(1, GLA_HEAD_V), lvl)


LOG2E = 1.4426950408889634
C_PIECES = 3


def _split3(x):
    hi = x.astype(BF16)
    r = x - hi.astype(F32)
    mid = r.astype(BF16)
    lo = (r - mid.astype(F32)).astype(BF16)
    return hi, mid, lo


def _fox_aug_matrices():
    n = FOX_HEADS * LANES
    pq = np.zeros((LANES, n), np.float32)
    pk = np.zeros((LANES, n), np.float32)
    oq = np.zeros((1, n), np.float32)
    ok = np.zeros((1, n), np.float32)
    for h in range(FOX_HEADS):
        o = h * LANES + (FOX_HEAD_DIM if h % 2 == 0 else 0)
        for t in range(C_PIECES):
            pq[FOX_HEADS * t + h, o + t] = 1.0
            ok[0, o + t] = 1.0
            pk[FOX_HEADS * t + h, o + C_PIECES + t] = -1.0
            oq[0, o + C_PIECES + t] = 1.0
    return pq, pk, oq, ok


def _fox_prep_kernel(q_ref, k_ref, v_ref, fl_ref, bf_ref, gq_ref, gk_ref, bd_ref, tri_ref,
                     pq_ref, pk_ref, oq_ref, ok_ref, qop_ref, kop_ref, vt_ref, carry_ref):
    @pl.when(pl.program_id(1) == 0)
    def _():
        carry_ref[...] = jnp.zeros_like(carry_ref)

    lf = _log_sigmoid(fl_ref[...] + bf_ref[...])
    hi, mid, lo = _split3(lf)
    tri = tri_ref[...]
    c = carry_ref[...] + (_dot(tri, hi) + _dot(tri, mid) + _dot(tri, lo))
    carry_ref[...] = c[c.shape[0] - 1:, :]

    lane = lax.broadcasted_iota(jnp.int32, (1, LANES), 1)
    hi, mid, lo = _split3(jnp.where(lane < FOX_HEADS, c * LOG2E, 0.0))
    cpack = (hi.astype(F32) + pltpu.roll(mid.astype(F32), FOX_HEADS, 1)
             + pltpu.roll(lo.astype(F32), 2 * FOX_HEADS, 1)).astype(BF16)
    aug_q = _dot(cpack, pq_ref[...]) + oq_ref[...]
    aug_k = _dot(cpack, pk_ref[...]) + ok_ref[...]

    bd = bd_ref[...]

    def head_norm(x, gain):
        sq = x * x
        hi, mid, lo = _split3(sq)
        ss = _dot(hi, bd) + _dot(mid, bd) + _dot(lo, bd)
        return x * lax.rsqrt(ss * (1.0 / FOX_HEAD_DIM) + EPS) * gain

    halves = (lane < FOX_HEAD_DIM, lane >= FOX_HEAD_DIM)
    for pair in range(FOX_PAIRS):
        sl = slice(pair * LANES, (pair + 1) * LANES)
        qn = head_norm(q_ref[:, sl], gq_ref[:, sl]) * (FOX_HEAD_DIM ** -0.5 * LOG2E)
        kn = head_norm(k_ref[:, sl], gk_ref[:, sl])
        for e in range(2):
            hs = slice((2 * pair + e) * LANES, (2 * pair + e + 1) * LANES)
            qop_ref[:, hs] = jnp.where(halves[e], qn, aug_q[:, hs]).astype(BF16)
            kop_ref[:, hs] = jnp.where(halves[e], kn, aug_k[:, hs]).astype(BF16)
        vt_ref[0, sl, :] = v_ref[:, sl].T.astype(BF16)


def _fox_prep(proj, fl, b_f, g_q, g_k, batch, seq, ts):
    t = batch * seq
    nblk = seq // ts
    d = D_MODEL
    n_op = FOX_HEADS * LANES
    bd = jnp.asarray(np.kron(np.eye(LANES // FOX_HEAD_DIM), np.ones((FOX_HEAD_DIM, FOX_HEAD_DIM))), BF16)
    tri = jnp.asarray(np.tril(np.ones((ts, ts))), BF16)
    pq, pk, oq, ok = _fox_aug_matrices()
    bf = jnp.zeros((1, LANES), F32).at[0, :FOX_HEADS].set(b_f)
    gq = jnp.tile(g_q, FOX_HEADS).reshape(1, d)
    gk = jnp.tile(g_k, FOX_HEADS).reshape(1, d)
    rowblk = lambda b, s: b * nblk + s
    const = lambda b, s: (0, 0)
    return pl.pallas_call(
        _fox_prep_kernel,
        name="fox_prep",
        grid=(batch, nblk),
        in_specs=[
            pl.BlockSpec((ts, d), lambda b, s: (rowblk(b, s), 0)),
            pl.BlockSpec((ts, d), lambda b, s: (rowblk(b, s), 1)),
            pl.BlockSpec((ts, d), lambda b, s: (rowblk(b, s), 2)),
            pl.BlockSpec((ts, LANES), lambda b, s: (rowblk(b, s), 0)),
            pl.BlockSpec((1, LANES), const),
            pl.BlockSpec((1, d), const),
            pl.BlockSpec((1, d), const),
            pl.BlockSpec((LANES, LANES), const),
            pl.BlockSpec((ts, ts), const),
            pl.BlockSpec((LANES, n_op), const),
            pl.BlockSpec((LANES, n_op), const),
            pl.BlockSpec((1, n_op), const),
            pl.BlockSpec((1, n_op), const),
        ],
        out_specs=[
            pl.BlockSpec((ts, n_op), lambda b, s: (rowblk(b, s), 0)),
            pl.BlockSpec((ts, n_op), lambda b, s: (rowblk(b, s), 0)),
            pl.BlockSpec((1, d, ts), lambda b, s: (b, 0, s)),
        ],
        out_shape=[
            jax.ShapeDtypeStruct((t, n_op), BF16),
            jax.ShapeDtypeStruct((t, n_op), BF16),
            jax.ShapeDtypeStruct((batch, d, seq), BF16),
        ],
        scratch_shapes=[pltpu.VMEM((1, LANES), F32)],
        compiler_params=_cparams(("parallel", "arbitrary")),
    )(proj, proj, proj, fl, bf, gq, gk, bd, tri,
      jnp.asarray(pq, BF16), jnp.asarray(pk, BF16), jnp.asarray(oq), jnp.asarray(ok))


def _fox_attn_kernel(q_ref, k_ref, vt_ref, gate_ref, o_ref, acc_ref, *, tq):
    qi = pl.program_id(2)
    half = FOX_HEAD_DIM
    sub = lax.broadcasted_iota(jnp.int32, (LANES, 1), 0)
    own_rows = (sub < half, sub >= half)
    qq = [q_ref[:, e * LANES:(e + 1) * LANES] for e in range(2)]

    acc_ref[...] = jnp.zeros_like(acc_ref)

    def block(kj, m, diagonal):
        start = pl.multiple_of(kj * tq, tq)
        vt = vt_ref[0, :, pl.ds(start, tq)]
        st = [_dot_nt(k_ref[pl.ds(start, tq), e * LANES:(e + 1) * LANES], qq[e]) for e in range(2)]
        m_out = []
        for e in range(2):
            s = st[e]
            if diagonal:
                key = lax.broadcasted_iota(jnp.int32, (tq, tq), 0)
                qry = lax.broadcasted_iota(jnp.int32, (tq, tq), 1)
                s = jnp.where(key <= qry, s, -jnp.inf)
            m_new = jnp.maximum(m[e], jnp.max(s, axis=0, keepdims=True))
            alpha = jnp.exp2(m[e] - m_new)
            p = jnp.exp2(s - m_new).astype(BF16)
            vop = jnp.where(own_rows[e], vt, jnp.ones_like(vt))
            acc_ref[e] = acc_ref[e] * alpha + _dot(vop, p)
            m_out.append(m_new)
        return tuple(m_out)

    m0 = jnp.full((1, tq), -1e30, F32)
    m = lax.fori_loop(0, qi, lambda kj, m: block(kj, m, False), (m0, m0))
    block(qi, m, True)

    outs = []
    for e in range(2):
        acc = acc_ref[e]
        outs.append(acc[e * half:(e + 1) * half] / acc[(1 - e) * half:(2 - e) * half])
    o2 = jnp.concatenate(outs, axis=0).T
    o_ref[...] = (o2 * _sigmoid(gate_ref[...])).astype(o_ref.dtype)


def _fox_attn(qop, kop, vt, proj, batch, seq, tq):
    t = batch * seq
    nq = seq // tq
    gate_blk = 3 * D_MODEL // LANES
    return pl.pallas_call(
        functools.partial(_fox_attn_kernel, tq=tq),
        name="fox_attn",
        grid=(batch, FOX_PAIRS, nq),
        in_specs=[
            pl.BlockSpec((tq, 2 * LANES), lambda b, hp, i: (b * nq + i, hp)),
            pl.BlockSpec((seq, 2 * LANES), lambda b, hp, i: (b, hp)),
            pl.BlockSpec((1, LANES, seq), lambda b, hp, i: (b, hp, 0)),
            pl.BlockSpec((tq, LANES), lambda b, hp, i: (b * nq + i, gate_blk + hp)),
        ],
        out_specs=pl.BlockSpec((tq, LANES), lambda b, hp, i: (b * nq + i, hp)),
        out_shape=jax.ShapeDtypeStruct((t, D_MODEL), BF16),
        scratch_shapes=[pltpu.VMEM((2, LANES, tq), F32)],
        compiler_params=_cparams(("parallel", "parallel", "arbitrary")),
    )(qop, kop, vt, proj)


def kernel(x, l0_norm_mix, l0_w_in, l0_w_gk_up, l0_b_gk, l0_g_onorm, l0_w_out, l0_norm_ffn, l0_w_ff1, l0_w_ff2, l1_norm_mix, l1_w_in, l1_b_f, l1_g_q, l1_g_k, l1_w_out, l1_norm_ffn, l1_w_ff1, l1_w_ff2, final_norm):
    batch, seq, d = x.shape
    t = batch * seq
    xf = x.reshape(t, d)

    n_main = 2 * GLA_DK + 2 * GLA_DV
    w_main = l0_w_in[:, :n_main].astype(BF16)
    w_gate = jnp.zeros((d, LANES), F32).at[:, :GLA_GATE_RANK].set(l0_w_in[:, n_main:]).astype(BF16)
    w_up = jnp.zeros((LANES, GLA_DK), F32).at[:GLA_GATE_RANK].set(l0_w_gk_up)
    proj = _norm_matmul(xf, l0_norm_mix, w_main, 1024, 512)
    gl = _norm_matmul(xf, l0_norm_mix, w_gate, 1024, LANES)
    og = _gla(proj, gl, w_up, l0_b_gk, l0_g_onorm, batch, seq)
    xf = _matmul_residual(og, l0_w_out.astype(BF16), xf, 1024)
    xf = _mlp(xf, l0_norm_ffn, l0_w_ff1.astype(BF16), l0_w_ff2.astype(BF16), final_norm, 1024, 512, False)

    w_qkv = l1_w_in[:, :3 * d]
    w_f = l1_w_in[:, 3 * d:3 * d + FOX_HEADS]
    w_g = l1_w_in[:, 3 * d + FOX_HEADS:]
    w_main = jnp.concatenate([w_qkv, w_g], axis=1).astype(BF16)
    w_fp = jnp.zeros((d, LANES), F32).at[:, :FOX_HEADS].set(w_f).astype(BF16)
    proj = _norm_matmul(xf, l1_norm_mix, w_main, 1024, 512)
    fl = _norm_matmul(xf, l1_norm_mix, w_fp, 1024, LANES)
    qop, kop, vt = _fox_prep(proj, fl, l1_b_f, l1_g_q, l1_g_k, batch, seq, 512)
    oa = _fox_attn(qop, kop, vt, proj, batch, seq, 512)
    xf = _matmul_residual(oa, l1_w_out.astype(BF16), xf, 1024)
    xf = _mlp(xf, l1_norm_ffn, l1_w_ff1.astype(BF16), l1_w_ff2.astype(BF16), final_norm, 1024, 512, True)
    return xf.reshape(batch, seq, d)
```

```python
import functools

import numpy as np
import jax
import jax.numpy as jnp
from jax import lax
from jax.experimental import pallas as pl
from jax.experimental.pallas import tpu as pltpu

F32 = jnp.float32
BF16 = jnp.bfloat16

EPS = 1e-6
D_MODEL = 1024
D_FF = 4 * D_MODEL

GLA_HEADS = 4
GLA_DK = 512
GLA_DV = 1024
GLA_HEAD_K = 128
GLA_HEAD_V = 256
GLA_GATE_RANK = 16
GLA_GATE_NORM = 16.0
GLA_BLOCK = 256
GLA_LEVELS = 8

FOX_HEADS = 16
FOX_HEAD_DIM = 64
FOX_PAIRS = FOX_HEADS // 2

LANES = 128
VMEM_LIMIT = 56 * 1024 * 1024


def _cparams(sem):
    return pltpu.CompilerParams(dimension_semantics=sem, vmem_limit_bytes=VMEM_LIMIT)


def _dot(a, b):
    return jnp.dot(a, b, preferred_element_type=F32)


def _dot_nt(a, b):
    return lax.dot_general(a, b, (((1,), (1,)), ((), ())), preferred_element_type=F32)


def _dot_tn(a, b):
    return lax.dot_general(a, b, (((0,), (0,)), ((), ())), preferred_element_type=F32)


def _rms(x, g):
    ms = jnp.mean(x * x, axis=-1, keepdims=True)
    return x * lax.rsqrt(ms + EPS) * g


def _log_sigmoid(z):
    return jnp.minimum(z, 0.0) - jnp.log1p(jnp.exp(-jnp.abs(z)))


def _sigmoid(z):
    return 1.0 / (1.0 + jnp.exp(-z))


def _norm_mm_kernel(x_ref, g_ref, w_ref, o_ref, xn_ref):
    @pl.when(pl.program_id(1) == 0)
    def _():
        xn_ref[...] = _rms(x_ref[...], g_ref[...]).astype(BF16)

    o_ref[...] = _dot(xn_ref[...], w_ref[...])


def _norm_matmul(x, g, w, tm, tn):
    t, d = x.shape
    n = w.shape[1]
    return pl.pallas_call(
        _norm_mm_kernel,
        name="norm_matmul",
        grid=(t // tm, n // tn),
        in_specs=[
            pl.BlockSpec((tm, d), lambda i, j: (i, 0)),
            pl.BlockSpec((1, d), lambda i, j: (0, 0)),
            pl.BlockSpec((d, tn), lambda i, j: (0, j)),
        ],
        out_specs=pl.BlockSpec((tm, tn), lambda i, j: (i, j)),
        out_shape=jax.ShapeDtypeStruct((t, n), F32),
        scratch_shapes=[pltpu.VMEM((tm, d), BF16)],
        compiler_params=_cparams(("parallel", "arbitrary")),
    )(x, g.reshape(1, d), w)


def _mm_res_kernel(a_ref, w_ref, x_ref, o_ref):
    o_ref[...] = x_ref[...] + _dot(a_ref[...], w_ref[...])


def _matmul_residual(a, w, x, tm):
    t, d = x.shape
    k = a.shape[1]
    return pl.pallas_call(
        _mm_res_kernel,
        name="matmul_residual",
        grid=(t // tm,),
        in_specs=[
            pl.BlockSpec((tm, k), lambda i: (i, 0)),
            pl.BlockSpec((k, d), lambda i: (0, 0)),
            pl.BlockSpec((tm, d), lambda i: (i, 0)),
        ],
        out_specs=pl.BlockSpec((tm, d), lambda i: (i, 0)),
        out_shape=jax.ShapeDtypeStruct((t, d), F32),
        compiler_params=_cparams(("parallel",)),
    )(a, w, x)


def _mlp_kernel(x_ref, g_ref, w1_ref, w2_ref, gf_ref, o_ref, xn_ref, acc_ref, *, final_norm):
    j = pl.program_id(1)

    @pl.when(j == 0)
    def _():
        xn_ref[...] = _rms(x_ref[...], g_ref[...]).astype(BF16)
        acc_ref[...] = jnp.zeros_like(acc_ref)

    h = _dot(xn_ref[...], w1_ref[...])
    h = jnp.square(jnp.maximum(h, 0.0)).astype(BF16)
    acc_ref[...] += _dot(h, w2_ref[...])

    @pl.when(j == pl.num_programs(1) - 1)
    def _():
        y = x_ref[...] + acc_ref[...]
        if final_norm:
            y = _rms(y, gf_ref[...])
        o_ref[...] = y


def _mlp(x, g, w1, w2, gf, tm, tf, final_norm):
    t, d = x.shape
    ff = w1.shape[1]
    return pl.pallas_call(
        functools.partial(_mlp_kernel, final_norm=final_norm),
        name="mlp",
        grid=(t // tm, ff // tf),
        in_specs=[
            pl.BlockSpec((tm, d), lambda i, j: (i, 0)),
            pl.BlockSpec((1, d), lambda i, j: (0, 0)),
            pl.BlockSpec((d, tf), lambda i, j: (0, j)),
            pl.BlockSpec((tf, d), lambda i, j: (j, 0)),
            pl.BlockSpec((1, d), lambda i, j: (0, 0)),
        ],
        out_specs=pl.BlockSpec((tm, d), lambda i, j: (i, 0)),
        out_shape=jax.ShapeDtypeStruct((t, d), F32),
        scratch_shapes=[pltpu.VMEM((tm, d), BF16), pltpu.VMEM((tm, d), F32)],
        compiler_params=_cparams(("parallel", "arbitrary")),
    )(x, g.reshape(1, d), w1, w2, gf.reshape(1, d))


def _gla_level_map(n):
    i = np.arange(n)[:, None]
    j = np.arange(n)[None, :]
    x = i ^ j
    lvl = np.zeros((n, n), np.int32)
    nz = x > 0
    lvl[nz] = np.floor(np.log2(x[nz])).astype(np.int32) + 1
    lvl = np.where(i > j, lvl, 0)
    lvl = np.where(i == j, GLA_LEVELS + 1, lvl)
    return lvl.astype(np.int32)


def _gla_kernel(q_ref, k_ref, v_ref, r_ref, gl_ref, wup_ref, bgk_ref, gon_ref, lvl_ref,
                o_ref, st_ref):
    L = GLA_BLOCK

    @pl.when(pl.program_id(2) == 0)
    def _():
        st_ref[...] = jnp.zeros_like(st_ref)

    z = _dot(gl_ref[...], wup_ref[...]) + bgk_ref[...]
    g = _log_sigmoid(z) * (1.0 / GLA_GATE_NORM)

    qs = q_ref[...] * (GLA_HEAD_K ** -0.5)
    k = k_ref[...]
    lvl = lvl_ref[...]
    row = lax.broadcasted_iota(jnp.int32, (L, GLA_HEAD_K), 0)

    a = jnp.where(lvl == GLA_LEVELS + 1, _dot_nt(qs.astype(BF16), k.astype(BF16)), 0.0)
    p = g
    tot = g
    for t in range(GLA_LEVELS):
        h = 1 << t
        upper = (row & h) != 0
        e = jnp.where(upper, p, tot - p)
        xs = (jnp.where(upper, qs, k) * jnp.exp(e)).astype(BF16)
        a = jnp.where(lvl == t + 1, _dot_nt(xs, xs), a)
        t_lo = pltpu.roll(tot, h, 0)
        t_hi = pltpu.roll(tot, L - h, 0)
        p = p + jnp.where(upper, t_lo, 0.0)
        tot = tot + jnp.where(upper, t_lo, t_hi)

    st = st_ref[...]
    v = v_ref[...].astype(BF16)
    qe = (qs * jnp.exp(p)).astype(BF16)
    o = _dot_nt(qe, st.astype(BF16)) + _dot(a.astype(BF16), v)
    kd = (k * jnp.exp(tot - p)).astype(BF16)
    st_ref[...] = st * jnp.exp(tot[0:1, :]) + _dot_tn(v, kd)

    r = r_ref[...]
    o_ref[...] = (_rms(o, gon_ref[...]) * (r * _sigmoid(r))).astype(o_ref.dtype)


def _gla(proj, gl, wup, bgk, gon, batch, seq):
    L = GLA_BLOCK
    nblk = seq // L
    t = batch * seq
    lvl = jnp.asarray(_gla_level_map(L))
    kb = GLA_DK // GLA_HEAD_K
    vb = 2 * GLA_DK // GLA_HEAD_V
    rb = vb + GLA_HEADS
    rowblk = lambda b, h, s: b * nblk + s
    return pl.pallas_call(
        _gla_kernel,
        name="gla",
        grid=(batch, GLA_HEADS, nblk),
        in_specs=[
            pl.BlockSpec((L, GLA_HEAD_K), lambda b, h, s: (rowblk(b, h, s), h)),
            pl.BlockSpec((L, GLA_HEAD_K), lambda b, h, s: (rowblk(b, h, s), kb + h)),
            pl.BlockSpec((L, GLA_HEAD_V), lambda b, h, s: (rowblk(b, h, s), vb + h)),
            pl.BlockSpec((L, GLA_HEAD_V), lambda b, h, s: (rowblk(b, h, s), rb + h)),
            pl.BlockSpec((L, LANES), lambda b, h, s: (rowblk(b, h, s), 0)),
            pl.BlockSpec((LANES, GLA_HEAD_K), lambda b, h, s: (0, h)),
            pl.BlockSpec((1, GLA_HEAD_K), lambda b, h, s: (0, h)),
            pl.BlockSpec((1, GLA_HEAD_V), lambda b, h, s: (0, 0)),
            pl.BlockSpec((L, L), lambda b, h, s: (0, 0)),
        ],
        out_specs=pl.BlockSpec((L, GLA_HEAD_V), lambda b, h, s: (rowblk(b, h, s), h)),
        out_shape=jax.ShapeDtypeStruct((t, GLA_DV), BF16),
        scratch_shapes=[pltpu.VMEM((GLA_HEAD_V, GLA_HEAD_K), F32)],
        compiler_params=_cparams(("parallel", "parallel", "arbitrary")),
    )(proj, proj, proj, proj, gl, wup, bgk.reshape(1, GLA_DK), gon.reshape(1, GLA_HEAD_V), lvl)


LOG2E = 1.4426950408889634
C_PIECES = 3


def _split3(x):
    hi = x.astype(BF16)
    r = x - hi.astype(F32)
    mid = r.astype(BF16)
    lo = (r - mid.astype(F32)).astype(BF16)
    return hi, mid, lo


def _fox_aug_matrices():
    n = FOX_HEADS * LANES
    pq = np.zeros((LANES, n), np.float32)
    pk = np.zeros((LANES, n), np.float32)
    oq = np.zeros((1, n), np.float32)
    ok = np.zeros((1, n), np.float32)
    for h in range(FOX_HEADS):
        o = h * LANES + (FOX_HEAD_DIM if h % 2 == 0 else 0)
        for t in range(C_PIECES):
            pq[FOX_HEADS * t + h, o + t] = 1.0
            ok[0, o + t] = 1.0
            pk[FOX_HEADS * t + h, o + C_PIECES + t] = -1.0
            oq[0, o + C_PIECES + t] = 1.0
    return pq, pk, oq, ok


def _fox_prep_kernel(q_ref, k_ref, v_ref, fl_ref, bf_ref, gq_ref, gk_ref, bd_ref, tri_ref,
                     pq_ref, pk_ref, oq_ref, ok_ref, qop_ref, kop_ref, vt_ref, carry_ref):
    @pl.when(pl.program_id(1) == 0)
    def _():
        carry_ref[...] = jnp.zeros_like(carry_ref)

    lf = _log_sigmoid(fl_ref[...] + bf_ref[...])
    hi, mid, lo = _split3(lf)
    tri = tri_ref[...]
    c = carry_ref[...] + (_dot(tri, hi) + _dot(tri, mid) + _dot(tri, lo))
    carry_ref[...] = c[c.shape[0] - 1:, :]

    lane = lax.broadcasted_iota(jnp.int32, (1, LANES), 1)
    hi, mid, lo = _split3(jnp.where(lane < FOX_HEADS, c * LOG2E, 0.0))
    cpack = (hi.astype(F32) + pltpu.roll(mid.astype(F32), FOX_HEADS, 1)
             + pltpu.roll(lo.astype(F32), 2 * FOX_HEADS, 1)).astype(BF16)
    aug_q = _dot(cpack, pq_ref[...]) + oq_ref[...]
    aug_k = _dot(cpack, pk_ref[...]) + ok_ref[...]

    bd = bd_ref[...]

    def head_norm(x, gain):
        sq = x * x
        hi, mid, lo = _split3(sq)
        ss = _dot(hi, bd) + _dot(mid, bd) + _dot(lo, bd)
        return x * lax.rsqrt(ss * (1.0 / FOX_HEAD_DIM) + EPS) * gain

    halves = (lane < FOX_HEAD_DIM, lane >= FOX_HEAD_DIM)
    for pair in range(FOX_PAIRS):
        sl = slice(pair * LANES, (pair + 1) * LANES)
        qn = head_norm(q_ref[:, sl], gq_ref[:, sl]) * (FOX_HEAD_DIM ** -0.5 * LOG2E)
        kn = head_norm(k_ref[:, sl], gk_ref[:, sl])
        for e in range(2):
            hs = slice((2 * pair + e) * LANES, (2 * pair + e + 1) * LANES)
            qop_ref[:, hs] = jnp.where(halves[e], qn, aug_q[:, hs]).astype(BF16)
            kop_ref[:, hs] = jnp.where(halves[e], kn, aug_k[:, hs]).astype(BF16)
        vt_ref[0, sl, :] = v_ref[:, sl].T.astype(BF16)


def _fox_prep(proj, fl, b_f, g_q, g_k, batch, seq, ts):
    t = batch * seq
    nblk = seq // ts
    d = D_MODEL
    n_op = FOX_HEADS * LANES
    bd = jnp.asarray(np.kron(np.eye(LANES // FOX_HEAD_DIM), np.ones((FOX_HEAD_DIM, FOX_HEAD_DIM))), BF16)
    tri = jnp.asarray(np.tril(np.ones((ts, ts))), BF16)
    pq, pk, oq, ok = _fox_aug_matrices()
    bf = jnp.zeros((1, LANES), F32).at[0, :FOX_HEADS].set(b_f)
    gq = jnp.tile(g_q, FOX_HEADS).reshape(1, d)
    gk = jnp.tile(g_k, FOX_HEADS).reshape(1, d)
    rowblk = lambda b, s: b * nblk + s
    const = lambda b, s: (0, 0)
    return pl.pallas_call(
        _fox_prep_kernel,
        name="fox_prep",
        grid=(batch, nblk),
        in_specs=[
            pl.BlockSpec((ts, d), lambda b, s: (rowblk(b, s), 0)),
            pl.BlockSpec((ts, d), lambda b, s: (rowblk(b, s), 1)),
            pl.BlockSpec((ts, d), lambda b, s: (rowblk(b, s), 2)),
            pl.BlockSpec((ts, LANES), lambda b, s: (rowblk(b, s), 0)),
            pl.BlockSpec((1, LANES), const),
            pl.BlockSpec((1, d), const),
            pl.BlockSpec((1, d), const),
            pl.BlockSpec((LANES, LANES), const),
            pl.BlockSpec((ts, ts), const),
            pl.BlockSpec((LANES, n_op), const),
            pl.BlockSpec((LANES, n_op), const),
            pl.BlockSpec((1, n_op), const),
            pl.BlockSpec((1, n_op), const),
        ],
        out_specs=[
            pl.BlockSpec((ts, n_op), lambda b, s: (rowblk(b, s), 0)),
            pl.BlockSpec((ts, n_op), lambda b, s: (rowblk(b, s), 0)),
            pl.BlockSpec((1, d, ts), lambda b, s: (b, 0, s)),
        ],
        out_shape=[
            jax.ShapeDtypeStruct((t, n_op), BF16),
            jax.ShapeDtypeStruct((t, n_op), BF16),
            jax.ShapeDtypeStruct((batch, d, seq), BF16),
        ],
        scratch_shapes=[pltpu.VMEM((1, LANES), F32)],
        compiler_params=_cparams(("parallel", "arbitrary")),
    )(proj, proj, proj, fl, bf, gq, gk, bd, tri,
      jnp.asarray(pq, BF16), jnp.asarray(pk, BF16), jnp.asarray(oq), jnp.asarray(ok))


def _fox_attn_kernel(q_ref, k_ref, vt_ref, gate_ref, o_ref, acc_ref, m_ref, s_ref, *, tq):
    qi = pl.program_id(2)
    half = FOX_HEAD_DIM
    sub = lax.broadcasted_iota(jnp.int32, (LANES, 1), 0)
    own_rows = (sub < half, sub >= half)
    qq = [q_ref[:, e * LANES:(e + 1) * LANES] for e in range(2)]

    acc_ref[...] = jnp.zeros_like(acc_ref)
    m_ref[...] = jnp.full(m_ref.shape, -1e30, F32)

    def put_scores(kj, slot):
        start = pl.multiple_of(kj * tq, tq)
        for e in range(2):
            s_ref[slot, e] = _dot_nt(k_ref[pl.ds(start, tq), e * LANES:(e + 1) * LANES], qq[e])

    def block(kj, slot, diagonal):
        start = pl.multiple_of(kj * tq, tq)
        vt = vt_ref[0, :, pl.ds(start, tq)]
        for e in range(2):
            s = s_ref[slot, e]
            if diagonal:
                key = lax.broadcasted_iota(jnp.int32, (tq, tq), 0)
                qry = lax.broadcasted_iota(jnp.int32, (tq, tq), 1)
                s = jnp.where(key <= qry, s, -jnp.inf)
            m_prev = m_ref[e]
            m_new = jnp.maximum(m_prev, jnp.max(s, axis=0, keepdims=True))
            alpha = jnp.exp2(m_prev - m_new)
            p = jnp.exp2(s - m_new).astype(BF16)
            vop = jnp.where(own_rows[e], vt, jnp.ones_like(vt))
            acc_ref[e] = acc_ref[e] * alpha + _dot(vop, p)
            m_ref[e] = m_new

    put_scores(0, 0)

    def pair_body(i, carry):
        put_scores(2 * i + 1, 1)
        block(2 * i, 0, False)
        put_scores(2 * i + 2, 0)
        block(2 * i + 1, 1, False)
        return carry

    lax.fori_loop(0, qi // 2, pair_body, 0)

    @pl.when(qi % 2 == 0)
    def _():
        block(qi, 0, True)

    @pl.when(qi % 2 == 1)
    def _():
        put_scores(qi, 1)
        block(qi - 1, 0, False)
        block(qi, 1, True)

    outs = []
    for e in range(2):
        acc = acc_ref[e]
        outs.append(acc[e * half:(e + 1) * half] / acc[(1 - e) * half:(2 - e) * half])
    o2 = jnp.concatenate(outs, axis=0).T
    o_ref[...] = (o2 * _sigmoid(gate_ref[...])).astype(o_ref.dtype)


def _fox_attn(qop, kop, vt, proj, batch, seq, tq):
    t = batch * seq
    nq = seq // tq
    gate_blk = 3 * D_MODEL // LANES
    return pl.pallas_call(
        functools.partial(_fox_attn_kernel, tq=tq),
        name="fox_attn",
        grid=(batch, FOX_PAIRS, nq),
        in_specs=[
            pl.BlockSpec((tq, 2 * LANES), lambda b, hp, i: (b * nq + i, hp)),
            pl.BlockSpec((seq, 2 * LANES), lambda b, hp, i: (b, hp)),
            pl.BlockSpec((1, LANES, seq), lambda b, hp, i: (b, hp, 0)),
            pl.BlockSpec((tq, LANES), lambda b, hp, i: (b * nq + i, gate_blk + hp)),
        ],
        out_specs=pl.BlockSpec((tq, LANES), lambda b, hp, i: (b * nq + i, hp)),
        out_shape=jax.ShapeDtypeStruct((t, D_MODEL), BF16),
        scratch_shapes=[pltpu.VMEM((2, LANES, tq), F32), pltpu.VMEM((2, 1, tq), F32),
                        pltpu.VMEM((2, 2, tq, tq), F32)],
        compiler_params=_cparams(("parallel", "parallel", "arbitrary")),
    )(qop, kop, vt, proj)


def kernel(x, l0_norm_mix, l0_w_in, l0_w_gk_up, l0_b_gk, l0_g_onorm, l0_w_out, l0_norm_ffn, l0_w_ff1, l0_w_ff2, l1_norm_mix, l1_w_in, l1_b_f, l1_g_q, l1_g_k, l1_w_out, l1_norm_ffn, l1_w_ff1, l1_w_ff2, final_norm):
    batch, seq, d = x.shape
    t = batch * seq
    xf = x.reshape(t, d)

    n_main = 2 * GLA_DK + 2 * GLA_DV
    w_main = l0_w_in[:, :n_main].astype(BF16)
    w_gate = jnp.zeros((d, LANES), F32).at[:, :GLA_GATE_RANK].set(l0_w_in[:, n_main:]).astype(BF16)
    w_up = jnp.zeros((LANES, GLA_DK), F32).at[:GLA_GATE_RANK].set(l0_w_gk_up)
    proj = _norm_matmul(xf, l0_norm_mix, w_main, 1024, 1024)
    gl = _norm_matmul(xf, l0_norm_mix, w_gate, 1024, LANES)
    og = _gla(proj, gl, w_up, l0_b_gk, l0_g_onorm, batch, seq)
    xf = _matmul_residual(og, l0_w_out.astype(BF16), xf, 1024)
    xf = _mlp(xf, l0_norm_ffn, l0_w_ff1.astype(BF16), l0_w_ff2.astype(BF16), final_norm, 1024, 1024, False)

    w_qkv = l1_w_in[:, :3 * d]
    w_f = l1_w_in[:, 3 * d:3 * d + FOX_HEADS]
    w_g = l1_w_in[:, 3 * d + FOX_HEADS:]
    w_main = jnp.concatenate([w_qkv, w_g], axis=1).astype(BF16)
    w_fp = jnp.zeros((d, LANES), F32).at[:, :FOX_HEADS].set(w_f).astype(BF16)
    proj = _norm_matmul(xf, l1_norm_mix, w_main, 1024, 1024)
    fl = _norm_matmul(xf, l1_norm_mix, w_fp, 1024, LANES)
    qop, kop, vt = _fox_prep(proj, fl, l1_b_f, l1_g_q, l1_g_k, batch, seq, 512)
    oa = _fox_attn(qop, kop, vt, proj, batch, seq, 512)
    xf = _matmul_residual(oa, l1_w_out.astype(BF16), xf, 1024)
    xf = _mlp(xf, l1_norm_ffn, l1_w_ff1.astype(BF16), l1_w_ff2.astype(BF16), final_norm, 1024, 1024, True)
    return xf.reshape(batch, seq, d)
```

```python
import functools

import numpy as np
import jax
import jax.numpy as jnp
from jax import lax
from jax.experimental import pallas as pl
from jax.experimental.pallas import tpu as pltpu

F32 = jnp.float32
BF16 = jnp.bfloat16

EPS = 1e-6
D_MODEL = 1024
D_FF = 4 * D_MODEL

GLA_HEADS = 4
GLA_DK = 512
GLA_DV = 1024
GLA_HEAD_K = 128
GLA_HEAD_V = 256
GLA_GATE_RANK = 16
GLA_GATE_NORM = 16.0
GLA_BLOCK = 256
GLA_LEVELS = 8

FOX_HEADS = 16
FOX_HEAD_DIM = 64
FOX_PAIRS = FOX_HEADS // 2
FOX_ONES_ROWS = 16
FOX_VT_ROWS = FOX_HEAD_DIM + FOX_ONES_ROWS

LANES = 128
VMEM_LIMIT = 56 * 1024 * 1024


def _cparams(sem):
    return pltpu.CompilerParams(dimension_semantics=sem, vmem_limit_bytes=VMEM_LIMIT)


def _dot(a, b):
    return jnp.dot(a, b, preferred_element_type=F32)


def _dot_nt(a, b):
    return lax.dot_general(a, b, (((1,), (1,)), ((), ())), preferred_element_type=F32)


def _dot_tn(a, b):
    return lax.dot_general(a, b, (((0,), (0,)), ((), ())), preferred_element_type=F32)


def _rms(x, g):
    ms = jnp.mean(x * x, axis=-1, keepdims=True)
    return x * lax.rsqrt(ms + EPS) * g


def _log_sigmoid(z):
    return jnp.minimum(z, 0.0) - jnp.log(1.0 + jnp.exp(-jnp.abs(z)))


def _sigmoid(z):
    return 0.5 * jnp.tanh(0.5 * z) + 0.5


def _norm_mm_kernel(x_ref, g_ref, w_ref, ws_ref, o_ref, side_ref, xn_ref):
    @pl.when(pl.program_id(1) == 0)
    def _():
        xn_ref[...] = _rms(x_ref[...], g_ref[...]).astype(BF16)
        side_ref[...] = _dot(xn_ref[...], ws_ref[...])

    o_ref[...] = _dot(xn_ref[...], w_ref[...]).astype(o_ref.dtype)


def _norm_matmul(x, g, w, w_side, tm, tn):
    t, d = x.shape
    n = w.shape[1]
    ns = w_side.shape[1]
    return pl.pallas_call(
        _norm_mm_kernel,
        name="norm_matmul",
        grid=(t // tm, n // tn),
        in_specs=[
            pl.BlockSpec((tm, d), lambda i, j: (i, 0)),
            pl.BlockSpec((1, d), lambda i, j: (0, 0)),
            pl.BlockSpec((d, tn), lambda i, j: (0, j)),
            pl.BlockSpec((d, ns), lambda i, j: (0, 0)),
        ],
        out_specs=[
            pl.BlockSpec((tm, tn), lambda i, j: (i, j)),
            pl.BlockSpec((tm, ns), lambda i, j: (i, 0)),
        ],
        out_shape=[jax.ShapeDtypeStruct((t, n), BF16), jax.ShapeDtypeStruct((t, ns), F32)],
        scratch_shapes=[pltpu.VMEM((tm, d), BF16)],
        compiler_params=_cparams(("parallel", "arbitrary")),
    )(x, g.reshape(1, d), w, w_side)


def _mm_res_kernel(a_ref, w_ref, x_ref, o_ref):
    o_ref[...] = x_ref[...] + _dot(a_ref[...], w_ref[...])


def _matmul_residual(a, w, x, tm):
    t, d = x.shape
    k = a.shape[1]
    return pl.pallas_call(
        _mm_res_kernel,
        name="matmul_residual",
        grid=(t // tm,),
        in_specs=[
            pl.BlockSpec((tm, k), lambda i: (i, 0)),
            pl.BlockSpec((k, d), lambda i: (0, 0)),
            pl.BlockSpec((tm, d), lambda i: (i, 0)),
        ],
        out_specs=pl.BlockSpec((tm, d), lambda i: (i, 0)),
        out_shape=jax.ShapeDtypeStruct((t, d), F32),
        compiler_params=_cparams(("parallel",)),
    )(a, w, x)


def _mlp_kernel(x_ref, g_ref, w1_ref, w2_ref, gf_ref, o_ref, xn_ref, acc_ref, *, final_norm):
    j = pl.program_id(1)

    @pl.when(j == 0)
    def _():
        xn_ref[...] = _rms(x_ref[...], g_ref[...]).astype(BF16)
        acc_ref[...] = jnp.zeros_like(acc_ref)

    h = _dot(xn_ref[...], w1_ref[...])
    h = jnp.square(jnp.maximum(h, 0.0)).astype(BF16)
    acc_ref[...] += _dot(h, w2_ref[...])

    @pl.when(j == pl.num_programs(1) - 1)
    def _():
        y = x_ref[...] + acc_ref[...]
        if final_norm:
            y = _rms(y, gf_ref[...])
        o_ref[...] = y


def _mlp(x, g, w1, w2, gf, tm, tf, final_norm):
    t, d = x.shape
    ff = w1.shape[1]
    return pl.pallas_call(
        functools.partial(_mlp_kernel, final_norm=final_norm),
        name="mlp",
        grid=(t // tm, ff // tf),
        in_specs=[
            pl.BlockSpec((tm, d), lambda i, j: (i, 0)),
            pl.BlockSpec((1, d), lambda i, j: (0, 0)),
            pl.BlockSpec((d, tf), lambda i, j: (0, j)),
            pl.BlockSpec((tf, d), lambda i, j: (j, 0)),
            pl.BlockSpec((1, d), lambda i, j: (0, 0)),
        ],
        out_specs=pl.BlockSpec((tm, d), lambda i, j: (i, 0)),
        out_shape=jax.ShapeDtypeStruct((t, d), F32),
        scratch_shapes=[pltpu.VMEM((tm, d), BF16), pltpu.VMEM((tm, d), F32)],
        compiler_params=_cparams(("parallel", "arbitrary")),
    )(x, g.reshape(1, d), w1, w2, gf.reshape(1, d))


def _gla_level_map(n):
    i = np.arange(n)[:, None]
    j = np.arange(n)[None, :]
    x = i ^ j
    lvl = np.zeros((n, n), np.int32)
    nz = x > 0
    lvl[nz] = np.floor(np.log2(x[nz])).astype(np.int32) + 1
    lvl = np.where(i > j, lvl, 0)
    lvl = np.where(i == j, GLA_LEVELS, lvl)
    return lvl.astype(np.int32)


def _gla_kernel(q_ref, k_ref, v_ref, r_ref, gl_ref, wup_ref, bgk_ref, gon_ref, lvl_ref,
                o_ref, st_ref):
    L = GLA_BLOCK

    @pl.when(pl.program_id(2) == 0)
    def _():
        st_ref[...] = jnp.zeros_like(st_ref)

    z = _dot(gl_ref[...], wup_ref[...]) + bgk_ref[...]
    g = _log_sigmoid(z) * (1.0 / GLA_GATE_NORM)

    qs = q_ref[...].astype(F32) * (GLA_HEAD_K ** -0.5)
    k = k_ref[...].astype(F32)
    lvl = lvl_ref[...]
    H2 = L // 2
    SUB = 8
    row = lax.broadcasted_iota(jnp.int32, (L, GLA_HEAD_K), 0)

    def halves(x, h):
        lo = [x[r:r + h] for r in range(0, L, 2 * h)]
        up = [x[r + h:r + 2 * h] for r in range(0, L, 2 * h)]
        return lo, up

    def interleave(lo, up):
        return jnp.concatenate([piece for pair in zip(lo, up) for piece in pair], axis=0)

    def level_update(a_diag, t, xs):
        xs = xs.astype(BF16)
        return [jnp.where(lvl == t + 1, _dot_nt(xs[i * H2:(i + 1) * H2], xs[i * H2:(i + 1) * H2]), a_diag[i])
                for i in range(2)]

    qb, kb = qs.astype(BF16), k.astype(BF16)
    a_diag = [jnp.where(lvl == GLA_LEVELS, _dot_nt(qb[i * H2:(i + 1) * H2], kb[i * H2:(i + 1) * H2]), 0.0)
              for i in range(2)]
    p = g
    tot = g
    a_off = None
    for t in range(GLA_LEVELS):
        h = 1 << t
        if h < SUB:
            upper = (row & h) != 0
            e = jnp.where(upper, p, tot - p)
            a_diag = level_update(a_diag, t, jnp.where(upper, qs, k) * jnp.exp(e))
            t_lo = pltpu.roll(tot, h, 0)
            t_hi = pltpu.roll(tot, L - h, 0)
            p = p + jnp.where(upper, t_lo, 0.0)
            tot = tot + jnp.where(upper, t_lo, t_hi)
        else:
            p_lo, p_up = halves(p, h)
            t_lo, t_up = halves(tot, h)
            q_up = halves(qs, h)[1]
            k_lo = halves(k, h)[0]
            x_lo = [kk * jnp.exp(tl - pl_) for kk, tl, pl_ in zip(k_lo, t_lo, p_lo)]
            x_up = [qq * jnp.exp(pu) for qq, pu in zip(q_up, p_up)]
            if 2 * h < L:
                a_diag = level_update(a_diag, t, interleave(x_lo, x_up))
            else:
                a_off = _dot_nt(x_up[0].astype(BF16), x_lo[0].astype(BF16))
            t_new = [tl + tu for tl, tu in zip(t_lo, t_up)]
            p = interleave(p_lo, [pu + tl for pu, tl in zip(p_up, t_lo)])
            tot = interleave(t_new, t_new)

    st = st_ref[...]
    v = v_ref[...]
    qe = (qs * jnp.exp(p)).astype(BF16)
    a_top = a_diag[0].astype(BF16)
    a_bot = jnp.concatenate([a_off, a_diag[1]], axis=1).astype(BF16)
    o_intra = jnp.concatenate([_dot(a_top, v[:H2]), _dot(a_bot, v)], axis=0)
    o = _dot_nt(qe, st.astype(BF16)) + o_intra
    kd = (k * jnp.exp(tot - p)).astype(BF16)
    st_ref[...] = st * jnp.exp(tot[0:1, :]) + _dot_tn(v, kd)

    r = r_ref[...].astype(F32)
    o_ref[...] = (_rms(o, gon_ref[...]) * (r * _sigmoid(r))).astype(o_ref.dtype)


def _gla(proj, gl, wup, bgk, gon, batch, seq):
    L = GLA_BLOCK
    nblk = seq // L
    t = batch * seq
    lvl = jnp.asarray(_gla_level_map(L // 2))
    kb = GLA_DK // GLA_HEAD_K
    vb = 2 * GLA_DK // GLA_HEAD_V
    rb = vb + GLA_HEADS
    rowblk = lambda b, h, s: b * nblk + s
    return pl.pallas_call(
        _gla_kernel,
        name="gla",
        grid=(batch, GLA_HEADS, nblk),
        in_specs=[
            pl.BlockSpec((L, GLA_HEAD_K), lambda b, h, s: (rowblk(b, h, s), h)),
            pl.BlockSpec((L, GLA_HEAD_K), lambda b, h, s: (rowblk(b, h, s), kb + h)),
            pl.BlockSpec((L, GLA_HEAD_V), lambda b, h, s: (rowblk(b, h, s), vb + h)),
            pl.BlockSpec((L, GLA_HEAD_V), lambda b, h, s: (rowblk(b, h, s), rb + h)),
            pl.BlockSpec((L, LANES), lambda b, h, s: (rowblk(b, h, s), 0)),
            pl.BlockSpec((LANES, GLA_HEAD_K), lambda b, h, s: (0, h)),
            pl.BlockSpec((1, GLA_HEAD_K), lambda b, h, s: (0, h)),
            pl.BlockSpec((1, GLA_HEAD_V), lambda b, h, s: (0, 0)),
            pl.BlockSpec((L // 2, L // 2), lambda b, h, s: (0, 0)),
        ],
        out_specs=pl.BlockSpec((L, GLA_HEAD_V), lambda b, h, s: (rowblk(b, h, s), h)),
        out_shape=jax.ShapeDtypeStruct((t, GLA_DV), BF16),
        scratch_shapes=[pltpu.VMEM((GLA_HEAD_V, GLA_HEAD_K), F32)],
        compiler_params=_cparams(("parallel", "parallel", "arbitrary")),
    )(proj, proj, proj, proj, gl, wup, bgk.reshape(1, GLA_DK), gon.reshape(1, GLA_HEAD_V), lvl)


LOG2E = 1.4426950408889634
C_PIECES = 3


def _split3(x):
    hi = x.astype(BF16)
    r = x - hi.astype(F32)
    mid = r.astype(BF16)
    lo = (r - mid.astype(F32)).astype(BF16)
    return hi, mid, lo


def _fox_aug_matrices():
    n = FOX_HEADS * LANES
    pq = np.zeros((LANES, n), np.float32)
    pk = np.zeros((LANES, n), np.float32)
    oq = np.zeros((1, n), np.float32)
    ok = np.zeros((1, n), np.float32)
    for h in range(FOX_HEADS):
        o = h * LANES + (FOX_HEAD_DIM if h % 2 == 0 else 0)
        for t in range(C_PIECES):
            pq[FOX_HEADS * t + h, o + t] = 1.0
            ok[0, o + t] = 1.0
            pk[FOX_HEADS * t + h, o + C_PIECES + t] = -1.0
            oq[0, o + C_PIECES + t] = 1.0
    return pq, pk, oq, ok


def _fox_prep_kernel(q_ref, k_ref, v_ref, fl_ref, bf_ref, gq_ref, gk_ref, bd_ref, tri_ref,
                     pq_ref, pk_ref, oq_ref, ok_ref, qop_ref, kop_ref, vt_ref, carry_ref):
    @pl.when(pl.program_id(1) == 0)
    def _():
        carry_ref[...] = jnp.zeros_like(carry_ref)

    lf = _log_sigmoid(fl_ref[...] + bf_ref[...])
    hi, mid, lo = _split3(lf)
    tri = tri_ref[...]
    c = carry_ref[...] + (_dot(tri, hi) + _dot(tri, mid) + _dot(tri, lo))
    carry_ref[...] = c[c.shape[0] - 1:, :]

    lane = lax.broadcasted_iota(jnp.int32, (1, LANES), 1)
    hi, mid, lo = _split3(jnp.where(lane < FOX_HEADS, c * LOG2E, 0.0))
    cpack = (hi.astype(F32) + pltpu.roll(mid.astype(F32), FOX_HEADS, 1)
             + pltpu.roll(lo.astype(F32), 2 * FOX_HEADS, 1)).astype(BF16)
    aug_q = _dot(cpack, pq_ref[...]) + oq_ref[...]
    aug_k = _dot(cpack, pk_ref[...]) + ok_ref[...]

    bd = bd_ref[...]

    def head_norm(x, gain):
        sq = x * x
        hi = sq.astype(BF16)
        lo = (sq - hi.astype(F32)).astype(BF16)
        ss = _dot(hi, bd) + _dot(lo, bd)
        return x * lax.rsqrt(ss * (1.0 / FOX_HEAD_DIM) + EPS) * gain

    halves = (lane < FOX_HEAD_DIM, lane >= FOX_HEAD_DIM)
    for pair in range(FOX_PAIRS):
        sl = slice(pair * LANES, (pair + 1) * LANES)
        qn = head_norm(q_ref[:, sl].astype(F32), gq_ref[:, sl]) * (FOX_HEAD_DIM ** -0.5 * LOG2E)
        kn = head_norm(k_ref[:, sl].astype(F32), gk_ref[:, sl])
        for e in range(2):
            hs = slice((2 * pair + e) * LANES, (2 * pair + e + 1) * LANES)
            qop_ref[:, hs] = jnp.where(halves[e], qn, aug_q[:, hs]).astype(BF16)
            kop_ref[:, hs] = jnp.where(halves[e], kn, aug_k[:, hs]).astype(BF16)
        vt = v_ref[:, sl].astype(F32).T.astype(BF16)
        ones = jnp.ones((FOX_ONES_ROWS, vt.shape[1]), BF16)
        for e in range(2):
            r0 = (2 * pair + e) * FOX_VT_ROWS
            vt_ref[0, r0:r0 + FOX_HEAD_DIM, :] = vt[e * FOX_HEAD_DIM:(e + 1) * FOX_HEAD_DIM]
            vt_ref[0, r0 + FOX_HEAD_DIM:r0 + FOX_VT_ROWS, :] = ones


def _fox_prep(proj, fl, b_f, g_q, g_k, batch, seq, ts):
    t = batch * seq
    nblk = seq // ts
    d = D_MODEL
    n_op = FOX_HEADS * LANES
    bd = jnp.asarray(np.kron(np.eye(LANES // FOX_HEAD_DIM), np.ones((FOX_HEAD_DIM, FOX_HEAD_DIM))), BF16)
    tri = jnp.asarray(np.tril(np.ones((ts, ts))), BF16)
    pq, pk, oq, ok = _fox_aug_matrices()
    bf = jnp.zeros((1, LANES), F32).at[0, :FOX_HEADS].set(b_f)
    gq = jnp.tile(g_q, FOX_HEADS).reshape(1, d)
    gk = jnp.tile(g_k, FOX_HEADS).reshape(1, d)
    rowblk = lambda b, s: b * nblk + s
    const = lambda b, s: (0, 0)
    return pl.pallas_call(
        _fox_prep_kernel,
        name="fox_prep",
        grid=(batch, nblk),
        in_specs=[
            pl.BlockSpec((ts, d), lambda b, s: (rowblk(b, s), 0)),
            pl.BlockSpec((ts, d), lambda b, s: (rowblk(b, s), 1)),
            pl.BlockSpec((ts, d), lambda b, s: (rowblk(b, s), 2)),
            pl.BlockSpec((ts, LANES), lambda b, s: (rowblk(b, s), 0)),
            pl.BlockSpec((1, LANES), const),
            pl.BlockSpec((1, d), const),
            pl.BlockSpec((1, d), const),
            pl.BlockSpec((LANES, LANES), const),
            pl.BlockSpec((ts, ts), const),
            pl.BlockSpec((LANES, n_op), const),
            pl.BlockSpec((LANES, n_op), const),
            pl.BlockSpec((1, n_op), const),
            pl.BlockSpec((1, n_op), const),
        ],
        out_specs=[
            pl.BlockSpec((ts, n_op), lambda b, s: (rowblk(b, s), 0)),
            pl.BlockSpec((ts, n_op), lambda b, s: (rowblk(b, s), 0)),
            pl.BlockSpec((1, FOX_HEADS * FOX_VT_ROWS, ts), lambda b, s: (b, 0, s)),
        ],
        out_shape=[
            jax.ShapeDtypeStruct((t, n_op), BF16),
            jax.ShapeDtypeStruct((t, n_op), BF16),
            jax.ShapeDtypeStruct((batch, FOX_HEADS * FOX_VT_ROWS, seq), BF16),
        ],
        scratch_shapes=[pltpu.VMEM((1, LANES), F32)],
        compiler_params=_cparams(("parallel", "arbitrary")),
    )(proj, proj, proj, fl, bf, gq, gk, bd, tri,
      jnp.asarray(pq, BF16), jnp.asarray(pk, BF16), jnp.asarray(oq), jnp.asarray(ok))


def _fox_attn_kernel(q_ref, k_ref, vt_ref, gate_ref, o_ref, acc_ref, m_ref, s_ref, *, tq):
    qi = pl.program_id(2)
    half = FOX_HEAD_DIM
    qq = [q_ref[:, e * LANES:(e + 1) * LANES] for e in range(2)]

    acc_ref[...] = jnp.zeros_like(acc_ref)
    m_ref[...] = jnp.full(m_ref.shape, -1e30, F32)

    def put_scores(kj, slot):
        start = pl.multiple_of(kj * tq, tq)
        for e in range(2):
            s_ref[slot, e] = _dot_nt(k_ref[pl.ds(start, tq), e * LANES:(e + 1) * LANES], qq[e])

    def block(kj, slot, diagonal):
        start = pl.multiple_of(kj * tq, tq)
        vt = vt_ref[0, :, pl.ds(start, tq)]
        for e in range(2):
            s = s_ref[slot, e]
            if diagonal:
                key = lax.broadcasted_iota(jnp.int32, (tq, tq), 0)
                qry = lax.broadcasted_iota(jnp.int32, (tq, tq), 1)
                s = jnp.where(key <= qry, s, -jnp.inf)
            m_prev = m_ref[e]
            m_new = jnp.maximum(m_prev, jnp.max(s, axis=0, keepdims=True))
            alpha = jnp.exp2(m_prev - m_new)
            p = jnp.exp2(s - m_new).astype(BF16)
            vop = vt[e * FOX_VT_ROWS:(e + 1) * FOX_VT_ROWS]
            acc_ref[e] = acc_ref[e] * alpha + _dot(vop, p)
            m_ref[e] = m_new

    put_scores(0, 0)

    def pair_body(i, carry):
        put_scores(2 * i + 1, 1)
        block(2 * i, 0, False)
        put_scores(2 * i + 2, 0)
        block(2 * i + 1, 1, False)
        return carry

    lax.fori_loop(0, qi // 2, pair_body, 0)

    @pl.when(qi % 2 == 0)
    def _():
        block(qi, 0, True)

    @pl.when(qi % 2 == 1)
    def _():
        put_scores(qi, 1)
        block(qi - 1, 0, False)
        block(qi, 1, True)

    outs = []
    for e in range(2):
        acc = acc_ref[e]
        denom = acc[half:half + 1]
        outs.append(acc[:half] * (1.0 / denom))
    o2 = jnp.concatenate(outs, axis=0).T
    o_ref[...] = (o2 * _sigmoid(gate_ref[...].astype(F32))).astype(o_ref.dtype)


def _fox_attn(qop, kop, vt, proj, batch, seq, tq):
    t = batch * seq
    nq = seq // tq
    gate_blk = 3 * D_MODEL // LANES
    return pl.pallas_call(
        functools.partial(_fox_attn_kernel, tq=tq),
        name="fox_attn",
        grid=(batch, FOX_PAIRS, nq),
        in_specs=[
            pl.BlockSpec((tq, 2 * LANES), lambda b, hp, i: (b * nq + i, hp)),
            pl.BlockSpec((seq, 2 * LANES), lambda b, hp, i: (b, hp)),
            pl.BlockSpec((1, 2 * FOX_VT_ROWS, seq), lambda b, hp, i: (b, hp, 0)),
            pl.BlockSpec((tq, LANES), lambda b, hp, i: (b * nq + i, gate_blk + hp)),
        ],
        out_specs=pl.BlockSpec((tq, LANES), lambda b, hp, i: (b * nq + i, hp)),
        out_shape=jax.ShapeDtypeStruct((t, D_MODEL), BF16),
        scratch_shapes=[pltpu.VMEM((2, FOX_VT_ROWS, tq), F32), pltpu.VMEM((2, 1, tq), F32),
                        pltpu.VMEM((2, 2, tq, tq), F32)],
        compiler_params=_cparams(("parallel", "parallel", "arbitrary")),
    )(qop, kop, vt, proj)


def kernel(x, l0_norm_mix, l0_w_in, l0_w_gk_up, l0_b_gk, l0_g_onorm, l0_w_out, l0_norm_ffn, l0_w_ff1, l0_w_ff2, l1_norm_mix, l1_w_in, l1_b_f, l1_g_q, l1_g_k, l1_w_out, l1_norm_ffn, l1_w_ff1, l1_w_ff2, final_norm):
    batch, seq, d = x.shape
    t = batch * seq
    xf = x.reshape(t, d)

    n_main = 2 * GLA_DK + 2 * GLA_DV
    w_main = l0_w_in[:, :n_main].astype(BF16)
    w_gate = jnp.zeros((d, LANES), F32).at[:, :GLA_GATE_RANK].set(l0_w_in[:, n_main:]).astype(BF16)
    w_up = jnp.zeros((LANES, GLA_DK), F32).at[:GLA_GATE_RANK].set(l0_w_gk_up)
    proj, gl = _norm_matmul(xf, l0_norm_mix, w_main, w_gate, 1024, 1024)
    og = _gla(proj, gl, w_up, l0_b_gk, l0_g_onorm, batch, seq)
    xf = _matmul_residual(og, l0_w_out.astype(BF16), xf, 1024)
    xf = _mlp(xf, l0_norm_ffn, l0_w_ff1.astype(BF16), l0_w_ff2.astype(BF16), final_norm, 1024, 1024, False)

    w_qkv = l1_w_in[:, :3 * d]
    w_f = l1_w_in[:, 3 * d:3 * d + FOX_HEADS]
    w_g = l1_w_in[:, 3 * d + FOX_HEADS:]
    w_main = jnp.concatenate([w_qkv, w_g], axis=1).astype(BF16)
    w_fp = jnp.zeros((d, LANES), F32).at[:, :FOX_HEADS].set(w_f).astype(BF16)
    proj, fl = _norm_matmul(xf, l1_norm_mix, w_main, w_fp, 1024, 1024)
    qop, kop, vt = _fox_prep(proj, fl, l1_b_f, l1_g_q, l1_g_k, batch, seq, 512)
    oa = _fox_attn(qop, kop, vt, proj, batch, seq, 512)
    xf = _matmul_residual(oa, l1_w_out.astype(BF16), xf, 1024)
    xf = _mlp(xf, l1_norm_ffn, l1_w_ff1.astype(BF16), l1_w_ff2.astype(BF16), final_norm, 1024, 1024, True)
    return xf.reshape(batch, seq, d)
```

```python
import functools

import numpy as np
import jax
import jax.numpy as jnp
from jax import lax
from jax.experimental import pallas as pl
from jax.experimental.pallas import tpu as pltpu

F32 = jnp.float32
BF16 = jnp.bfloat16

EPS = 1e-6
D_MODEL = 1024
D_FF = 4 * D_MODEL

GLA_HEADS = 4
GLA_DK = 512
GLA_DV = 1024
GLA_HEAD_K = 128
GLA_HEAD_V = 256
GLA_GATE_RANK = 16
GLA_GATE_NORM = 16.0
GLA_BLOCK = 256
GLA_LEVELS = 8

FOX_HEADS = 16
FOX_HEAD_DIM = 64
FOX_PAIRS = FOX_HEADS // 2
FOX_ONES_ROWS = 16
FOX_VT_ROWS = FOX_HEAD_DIM + FOX_ONES_ROWS

LANES = 128
VMEM_LIMIT = 56 * 1024 * 1024


def _cparams(sem):
    return pltpu.CompilerParams(dimension_semantics=sem, vmem_limit_bytes=VMEM_LIMIT)


def _dot(a, b):
    return jnp.dot(a, b, preferred_element_type=F32)


def _dot_nt(a, b):
    return lax.dot_general(a, b, (((1,), (1,)), ((), ())), preferred_element_type=F32)


def _dot_tn(a, b):
    return lax.dot_general(a, b, (((0,), (0,)), ((), ())), preferred_element_type=F32)


def _rms(x, g):
    ms = jnp.mean(x * x, axis=-1, keepdims=True)
    return x * lax.rsqrt(ms + EPS) * g


def _log_sigmoid(z):
    return jnp.minimum(z, 0.0) - jnp.log(1.0 + jnp.exp(-jnp.abs(z)))


def _sigmoid(z):
    return 0.5 * jnp.tanh(0.5 * z) + 0.5


def _norm_mm_kernel(x_ref, g_ref, w_ref, ws_ref, o_ref, side_ref, *, tn):
    xn = _rms(x_ref[...], g_ref[...]).astype(BF16)
    side_ref[...] = _dot(xn, ws_ref[...])
    for c in range(w_ref.shape[1] // tn):
        cols = slice(c * tn, (c + 1) * tn)
        o_ref[:, cols] = _dot(xn, w_ref[:, cols]).astype(o_ref.dtype)


def _norm_matmul(x, g, w, w_side, tm, tn):
    t, d = x.shape
    n = w.shape[1]
    ns = w_side.shape[1]
    return pl.pallas_call(
        functools.partial(_norm_mm_kernel, tn=tn),
        name="norm_matmul",
        grid=(t // tm,),
        in_specs=[
            pl.BlockSpec((tm, d), lambda i: (i, 0)),
            pl.BlockSpec((1, d), lambda i: (0, 0)),
            pl.BlockSpec((d, n), lambda i: (0, 0)),
            pl.BlockSpec((d, ns), lambda i: (0, 0)),
        ],
        out_specs=[
            pl.BlockSpec((tm, n), lambda i: (i, 0)),
            pl.BlockSpec((tm, ns), lambda i: (i, 0)),
        ],
        out_shape=[jax.ShapeDtypeStruct((t, n), BF16), jax.ShapeDtypeStruct((t, ns), F32)],
        compiler_params=_cparams(("parallel",)),
    )(x, g.reshape(1, d), w, w_side)


def _mix_out_mlp_kernel(a_ref, wo_ref, x_ref, g_ref, w1_ref, w2_ref, gf_ref, o_ref, *, tf, final_norm):
    x1 = x_ref[...] + _dot(a_ref[...], wo_ref[...])
    xn = _rms(x1, g_ref[...]).astype(BF16)
    y = x1
    for c in range(w1_ref.shape[1] // tf):
        h = _dot(xn, w1_ref[:, c * tf:(c + 1) * tf])
        h = jnp.square(jnp.maximum(h, 0.0)).astype(BF16)
        y = y + _dot(h, w2_ref[c * tf:(c + 1) * tf, :])
    if final_norm:
        y = _rms(y, gf_ref[...])
    o_ref[...] = y


def _mix_out_mlp(a, w_out, x, g, w1, w2, gf, tm, tf, final_norm):
    t, d = x.shape
    ff = w1.shape[1]
    resident = lambda shape: pl.BlockSpec(shape, lambda i: (0, 0), pipeline_mode=pl.Buffered(1))
    return pl.pallas_call(
        functools.partial(_mix_out_mlp_kernel, tf=tf, final_norm=final_norm),
        name="mix_out_mlp",
        grid=(t // tm,),
        in_specs=[
            pl.BlockSpec((tm, d), lambda i: (i, 0)),
            resident((d, d)),
            pl.BlockSpec((tm, d), lambda i: (i, 0)),
            resident((1, d)),
            resident((d, ff)),
            resident((ff, d)),
            resident((1, d)),
        ],
        out_specs=pl.BlockSpec((tm, d), lambda i: (i, 0)),
        out_shape=jax.ShapeDtypeStruct((t, d), F32),
        compiler_params=_cparams(("parallel",)),
    )(a, w_out, x, g.reshape(1, d), w1, w2, gf.reshape(1, d))


def _gla_level_map(n):
    i = np.arange(n)[:, None]
    j = np.arange(n)[None, :]
    x = i ^ j
    lvl = np.zeros((n, n), np.int32)
    nz = x > 0
    lvl[nz] = np.floor(np.log2(x[nz])).astype(np.int32) + 1
    lvl = np.where(i > j, lvl, 0)
    lvl = np.where(i == j, GLA_LEVELS, lvl)
    return lvl.astype(np.int32)


def _gla_kernel(q_ref, k_ref, v_ref, r_ref, gl_ref, wup_ref, bgk_ref, gon_ref, lvl_ref,
                o_ref, st_ref):
    L = GLA_BLOCK

    @pl.when(pl.program_id(2) == 0)
    def _():
        st_ref[...] = jnp.zeros_like(st_ref)

    z = _dot(gl_ref[...], wup_ref[...]) + bgk_ref[...]
    g = _log_sigmoid(z) * (1.0 / GLA_GATE_NORM)

    qs = q_ref[...].astype(F32) * (GLA_HEAD_K ** -0.5)
    k = k_ref[...].astype(F32)
    lvl = lvl_ref[...]
    H2 = L // 2
    SUB = 8
    row = lax.broadcasted_iota(jnp.int32, (L, GLA_HEAD_K), 0)

    def halves(x, h):
        lo = [x[r:r + h] for r in range(0, L, 2 * h)]
        up = [x[r + h:r + 2 * h] for r in range(0, L, 2 * h)]
        return lo, up

    def interleave(lo, up):
        return jnp.concatenate([piece for pair in zip(lo, up) for piece in pair], axis=0)

    def level_update(a_diag, t, xs):
        xs = xs.astype(BF16)
        return [jnp.where(lvl == t + 1, _dot_nt(xs[i * H2:(i + 1) * H2], xs[i * H2:(i + 1) * H2]), a_diag[i])
                for i in range(2)]

    qb, kb = qs.astype(BF16), k.astype(BF16)
    a_diag = [jnp.where(lvl == GLA_LEVELS, _dot_nt(qb[i * H2:(i + 1) * H2], kb[i * H2:(i + 1) * H2]), 0.0)
              for i in range(2)]
    p = g
    tot = g
    a_off = None
    for t in range(GLA_LEVELS):
        h = 1 << t
        if h < SUB:
            upper = (row & h) != 0
            e = jnp.where(upper, p, tot - p)
            a_diag = level_update(a_diag, t, jnp.where(upper, qs, k) * jnp.exp(e))
            t_lo = pltpu.roll(tot, h, 0)
            t_hi = pltpu.roll(tot, L - h, 0)
            p = p + jnp.where(upper, t_lo, 0.0)
            tot = tot + jnp.where(upper, t_lo, t_hi)
        else:
            p_lo, p_up = halves(p, h)
            t_lo, t_up = halves(tot, h)
            q_up = halves(qs, h)[1]
            k_lo = halves(k, h)[0]
            x_lo = [kk * jnp.exp(tl - pl_) for kk, tl, pl_ in zip(k_lo, t_lo, p_lo)]
            x_up = [qq * jnp.exp(pu) for qq, pu in zip(q_up, p_up)]
            if 2 * h < L:
                a_diag = level_update(a_diag, t, interleave(x_lo, x_up))
            else:
                a_off = _dot_nt(x_up[0].astype(BF16), x_lo[0].astype(BF16))
            t_new = [tl + tu for tl, tu in zip(t_lo, t_up)]
            p = interleave(p_lo, [pu + tl for pu, tl in zip(p_up, t_lo)])
            tot = interleave(t_new, t_new)

    st = st_ref[...]
    v = v_ref[...]
    qe = (qs * jnp.exp(p)).astype(BF16)
    a_top = a_diag[0].astype(BF16)
    a_bot = jnp.concatenate([a_off, a_diag[1]], axis=1).astype(BF16)
    o_intra = jnp.concatenate([_dot(a_top, v[:H2]), _dot(a_bot, v)], axis=0)
    o = _dot_nt(qe, st.astype(BF16)) + o_intra
    kd = (k * jnp.exp(tot - p)).astype(BF16)
    st_ref[...] = st * jnp.exp(tot[0:1, :]) + _dot_tn(v, kd)

    r = r_ref[...].astype(F32)
    o_ref[...] = (_rms(o, gon_ref[...]) * (r * _sigmoid(r))).astype(o_ref.dtype)


def _gla(proj, gl, wup, bgk, gon, batch, seq):
    L = GLA_BLOCK
    nblk = seq // L
    t = batch * seq
    lvl = jnp.asarray(_gla_level_map(L // 2))
    kb = GLA_DK // GLA_HEAD_K
    vb = 2 * GLA_DK // GLA_HEAD_V
    rb = vb + GLA_HEADS
    rowblk = lambda b, h, s: b * nblk + s
    return pl.pallas_call(
        _gla_kernel,
        name="gla",
        grid=(batch, GLA_HEADS, nblk),
        in_specs=[
            pl.BlockSpec((L, GLA_HEAD_K), lambda b, h, s: (rowblk(b, h, s), h)),
            pl.BlockSpec((L, GLA_HEAD_K), lambda b, h, s: (rowblk(b, h, s), kb + h)),
            pl.BlockSpec((L, GLA_HEAD_V), lambda b, h, s: (rowblk(b, h, s), vb + h)),
            pl.BlockSpec((L, GLA_HEAD_V), lambda b, h, s: (rowblk(b, h, s), rb + h)),
            pl.BlockSpec((L, LANES), lambda b, h, s: (rowblk(b, h, s), 0)),
            pl.BlockSpec((LANES, GLA_HEAD_K), lambda b, h, s: (0, h)),
            pl.BlockSpec((1, GLA_HEAD_K), lambda b, h, s: (0, h)),
            pl.BlockSpec((1, GLA_HEAD_V), lambda b, h, s: (0, 0)),
            pl.BlockSpec((L // 2, L // 2), lambda b, h, s: (0, 0)),
        ],
        out_specs=pl.BlockSpec((L, GLA_HEAD_V), lambda b, h, s: (rowblk(b, h, s), h)),
        out_shape=jax.ShapeDtypeStruct((t, GLA_DV), BF16),
        scratch_shapes=[pltpu.VMEM((GLA_HEAD_V, GLA_HEAD_K), F32)],
        compiler_params=_cparams(("parallel", "parallel", "arbitrary")),
    )(proj, proj, proj, proj, gl, wup, bgk.reshape(1, GLA_DK), gon.reshape(1, GLA_HEAD_V), lvl)


LOG2E = 1.4426950408889634
C_PIECES = 3


def _split3(x):
    hi = x.astype(BF16)
    r = x - hi.astype(F32)
    mid = r.astype(BF16)
    lo = (r - mid.astype(F32)).astype(BF16)
    return hi, mid, lo


def _fox_aug_matrices():
    n = FOX_HEADS * LANES
    pq = np.zeros((LANES, n), np.float32)
    pk = np.zeros((LANES, n), np.float32)
    oq = np.zeros((1, n), np.float32)
    ok = np.zeros((1, n), np.float32)
    for h in range(FOX_HEADS):
        o = h * LANES + (FOX_HEAD_DIM if h % 2 == 0 else 0)
        for t in range(C_PIECES):
            pq[FOX_HEADS * t + h, o + t] = 1.0
            ok[0, o + t] = 1.0
            pk[FOX_HEADS * t + h, o + C_PIECES + t] = -1.0
            oq[0, o + C_PIECES + t] = 1.0
    return pq, pk, oq, ok


def _fox_prep_kernel(q_ref, k_ref, v_ref, fl_ref, bf_ref, gq_ref, gk_ref, bd_ref, tri_ref,
                     pq_ref, pk_ref, oq_ref, ok_ref, qop_ref, kop_ref, vt_ref, carry_ref):
    @pl.when(pl.program_id(1) == 0)
    def _():
        carry_ref[...] = jnp.zeros_like(carry_ref)

    lf = _log_sigmoid(fl_ref[...] + bf_ref[...])
    hi, mid, lo = _split3(lf)
    tri = tri_ref[...]
    c = carry_ref[...] + (_dot(tri, hi) + _dot(tri, mid) + _dot(tri, lo))
    carry_ref[...] = c[c.shape[0] - 1:, :]

    lane = lax.broadcasted_iota(jnp.int32, (1, LANES), 1)
    hi, mid, lo = _split3(jnp.where(lane < FOX_HEADS, c * LOG2E, 0.0))
    cpack = (hi.astype(F32) + pltpu.roll(mid.astype(F32), FOX_HEADS, 1)
             + pltpu.roll(lo.astype(F32), 2 * FOX_HEADS, 1)).astype(BF16)
    aug_q = _dot(cpack, pq_ref[...]) + oq_ref[...]
    aug_k = _dot(cpack, pk_ref[...]) + ok_ref[...]

    bd = bd_ref[...]

    def head_norm(x, gain):
        sq = x * x
        hi = sq.astype(BF16)
        lo = (sq - hi.astype(F32)).astype(BF16)
        ss = _dot(hi, bd) + _dot(lo, bd)
        return x * lax.rsqrt(ss * (1.0 / FOX_HEAD_DIM) + EPS) * gain

    halves = (lane < FOX_HEAD_DIM, lane >= FOX_HEAD_DIM)
    for pair in range(FOX_PAIRS):
        sl = slice(pair * LANES, (pair + 1) * LANES)
        qn = head_norm(q_ref[:, sl].astype(F32), gq_ref[:, sl]) * (FOX_HEAD_DIM ** -0.5 * LOG2E)
        kn = head_norm(k_ref[:, sl].astype(F32), gk_ref[:, sl])
        for e in range(2):
            hs = slice((2 * pair + e) * LANES, (2 * pair + e + 1) * LANES)
            qop_ref[:, hs] = jnp.where(halves[e], qn, aug_q[:, hs]).astype(BF16)
            kop_ref[:, hs] = jnp.where(halves[e], kn, aug_k[:, hs]).astype(BF16)
        vt = v_ref[:, sl].astype(F32).T.astype(BF16)
        ones = jnp.ones((FOX_ONES_ROWS, vt.shape[1]), BF16)
        for e in range(2):
            r0 = (2 * pair + e) * FOX_VT_ROWS
            vt_ref[0, r0:r0 + FOX_HEAD_DIM, :] = vt[e * FOX_HEAD_DIM:(e + 1) * FOX_HEAD_DIM]
            vt_ref[0, r0 + FOX_HEAD_DIM:r0 + FOX_VT_ROWS, :] = ones


def _fox_prep(proj, fl, b_f, g_q, g_k, batch, seq, ts):
    t = batch * seq
    nblk = seq // ts
    d = D_MODEL
    n_op = FOX_HEADS * LANES
    bd = jnp.asarray(np.kron(np.eye(LANES // FOX_HEAD_DIM), np.ones((FOX_HEAD_DIM, FOX_HEAD_DIM))), BF16)
    tri = jnp.asarray(np.tril(np.ones((ts, ts))), BF16)
    pq, pk, oq, ok = _fox_aug_matrices()
    bf = jnp.zeros((1, LANES), F32).at[0, :FOX_HEADS].set(b_f)
    gq = jnp.tile(g_q, FOX_HEADS).reshape(1, d)
    gk = jnp.tile(g_k, FOX_HEADS).reshape(1, d)
    rowblk = lambda b, s: b * nblk + s
    const = lambda b, s: (0, 0)
    return pl.pallas_call(
        _fox_prep_kernel,
        name="fox_prep",
        grid=(batch, nblk),
        in_specs=[
            pl.BlockSpec((ts, d), lambda b, s: (rowblk(b, s), 0)),
            pl.BlockSpec((ts, d), lambda b, s: (rowblk(b, s), 1)),
            pl.BlockSpec((ts, d), lambda b, s: (rowblk(b, s), 2)),
            pl.BlockSpec((ts, LANES), lambda b, s: (rowblk(b, s), 0)),
            pl.BlockSpec((1, LANES), const),
            pl.BlockSpec((1, d), const),
            pl.BlockSpec((1, d), const),
            pl.BlockSpec((LANES, LANES), const),
            pl.BlockSpec((ts, ts), const),
            pl.BlockSpec((LANES, n_op), const),
            pl.BlockSpec((LANES, n_op), const),
            pl.BlockSpec((1, n_op), const),
            pl.BlockSpec((1, n_op), const),
        ],
        out_specs=[
            pl.BlockSpec((ts, n_op), lambda b, s: (rowblk(b, s), 0)),
            pl.BlockSpec((ts, n_op), lambda b, s: (rowblk(b, s), 0)),
            pl.BlockSpec((1, FOX_HEADS * FOX_VT_ROWS, ts), lambda b, s: (b, 0, s)),
        ],
        out_shape=[
            jax.ShapeDtypeStruct((t, n_op), BF16),
            jax.ShapeDtypeStruct((t, n_op), BF16),
            jax.ShapeDtypeStruct((batch, FOX_HEADS * FOX_VT_ROWS, seq), BF16),
        ],
        scratch_shapes=[pltpu.VMEM((1, LANES), F32)],
        compiler_params=_cparams(("parallel", "arbitrary")),
    )(proj, proj, proj, fl, bf, gq, gk, bd, tri,
      jnp.asarray(pq, BF16), jnp.asarray(pk, BF16), jnp.asarray(oq), jnp.asarray(ok))


def _fox_attn_kernel(q_ref, k_ref, vt_ref, gate_ref, o_ref, acc_ref, m_ref, s_ref, *, tq):
    half = FOX_HEAD_DIM
    nq = q_ref.shape[0] // tq

    def q_block(qi):
        q_rows = pl.ds(pl.multiple_of(qi * tq, tq), tq)
        qq = [q_ref[q_rows, e * LANES:(e + 1) * LANES] for e in range(2)]

        acc_ref[...] = jnp.zeros_like(acc_ref)
        m_ref[...] = jnp.full(m_ref.shape, -1e30, F32)

        def put_scores(kj, slot):
            start = pl.multiple_of(kj * tq, tq)
            for e in range(2):
                s_ref[slot, e] = _dot_nt(k_ref[pl.ds(start, tq), e * LANES:(e + 1) * LANES], qq[e])

        def block(kj, slot, diagonal):
            start = pl.multiple_of(kj * tq, tq)
            vt = vt_ref[0, :, pl.ds(start, tq)]
            for e in range(2):
                s = s_ref[slot, e]
                if diagonal:
                    key = lax.broadcasted_iota(jnp.int32, (tq, tq), 0)
                    qry = lax.broadcasted_iota(jnp.int32, (tq, tq), 1)
                    s = jnp.where(key <= qry, s, -jnp.inf)
                m_prev = m_ref[e]
                m_new = jnp.maximum(m_prev, jnp.max(s, axis=0, keepdims=True))
                alpha = jnp.exp2(m_prev - m_new)
                p = jnp.exp2(s - m_new).astype(BF16)
                vop = vt[e * FOX_VT_ROWS:(e + 1) * FOX_VT_ROWS]
                acc_ref[e] = acc_ref[e] * alpha + _dot(vop, p)
                m_ref[e] = m_new

        put_scores(0, 0)

        def pair_body(i, carry):
            put_scores(2 * i + 1, 1)
            block(2 * i, 0, False)
            put_scores(2 * i + 2, 0)
            block(2 * i + 1, 1, False)
            return carry

        lax.fori_loop(0, qi // 2, pair_body, 0)

        @pl.when(qi % 2 == 0)
        def _():
            block(qi, 0, True)

        @pl.when(qi % 2 == 1)
        def _():
            put_scores(qi, 1)
            block(qi - 1, 0, False)
            block(qi, 1, True)

        outs = []
        for e in range(2):
            acc = acc_ref[e]
            denom = acc[half:half + 1]
            outs.append(acc[:half] * (1.0 / denom))
        o2 = jnp.concatenate(outs, axis=0).T
        o_ref[q_rows, :] = (o2 * _sigmoid(gate_ref[q_rows, :].astype(F32))).astype(o_ref.dtype)

    def q_body(qi, carry):
        q_block(qi)
        return carry

    lax.fori_loop(0, nq, q_body, 0)


def _fox_attn(qop, kop, vt, proj, batch, seq, tq):
    t = batch * seq
    gate_blk = 3 * D_MODEL // LANES
    return pl.pallas_call(
        functools.partial(_fox_attn_kernel, tq=tq),
        name="fox_attn",
        grid=(batch, FOX_PAIRS),
        in_specs=[
            pl.BlockSpec((seq, 2 * LANES), lambda b, hp: (b, hp)),
            pl.BlockSpec((seq, 2 * LANES), lambda b, hp: (b, hp)),
            pl.BlockSpec((1, 2 * FOX_VT_ROWS, seq), lambda b, hp: (b, hp, 0)),
            pl.BlockSpec((seq, LANES), lambda b, hp: (b, gate_blk + hp)),
        ],
        out_specs=pl.BlockSpec((seq, LANES), lambda b, hp: (b, hp)),
        out_shape=jax.ShapeDtypeStruct((t, D_MODEL), BF16),
        scratch_shapes=[pltpu.VMEM((2, FOX_VT_ROWS, tq), F32), pltpu.VMEM((2, 1, tq), F32),
                        pltpu.VMEM((2, 2, tq, tq), F32)],
        compiler_params=_cparams(("parallel", "parallel")),
    )(qop, kop, vt, proj)


def kernel(x, l0_norm_mix, l0_w_in, l0_w_gk_up, l0_b_gk, l0_g_onorm, l0_w_out, l0_norm_ffn, l0_w_ff1, l0_w_ff2, l1_norm_mix, l1_w_in, l1_b_f, l1_g_q, l1_g_k, l1_w_out, l1_norm_ffn, l1_w_ff1, l1_w_ff2, final_norm):
    batch, seq, d = x.shape
    t = batch * seq
    xf = x.reshape(t, d)

    n_main = 2 * GLA_DK + 2 * GLA_DV
    w_main = l0_w_in[:, :n_main].astype(BF16)
    w_gate = jnp.zeros((d, LANES), F32).at[:, :GLA_GATE_RANK].set(l0_w_in[:, n_main:]).astype(BF16)
    w_up = jnp.zeros((LANES, GLA_DK), F32).at[:GLA_GATE_RANK].set(l0_w_gk_up)
    proj, gl = _norm_matmul(xf, l0_norm_mix, w_main, w_gate, 512, 1024)
    og = _gla(proj, gl, w_up, l0_b_gk, l0_g_onorm, batch, seq)
    xf = _mix_out_mlp(og, l0_w_out.astype(BF16), xf, l0_norm_ffn, l0_w_ff1.astype(BF16),
                      l0_w_ff2.astype(BF16), final_norm, 512, 1024, False)

    w_qkv = l1_w_in[:, :3 * d]
    w_f = l1_w_in[:, 3 * d:3 * d + FOX_HEADS]
    w_g = l1_w_in[:, 3 * d + FOX_HEADS:]
    w_main = jnp.concatenate([w_qkv, w_g], axis=1).astype(BF16)
    w_fp = jnp.zeros((d, LANES), F32).at[:, :FOX_HEADS].set(w_f).astype(BF16)
    proj, fl = _norm_matmul(xf, l1_norm_mix, w_main, w_fp, 512, 1024)
    qop, kop, vt = _fox_prep(proj, fl, l1_b_f, l1_g_q, l1_g_k, batch, seq, 512)
    oa = _fox_attn(qop, kop, vt, proj, batch, seq, 512)
    xf = _mix_out_mlp(oa, l1_w_out.astype(BF16), xf, l1_norm_ffn, l1_w_ff1.astype(BF16),
                      l1_w_ff2.astype(BF16), final_norm, 512, 1024, True)
    return xf.reshape(batch, seq, d)
```

```python
import functools

import numpy as np
import jax
import jax.numpy as jnp
from jax import lax
from jax.experimental import pallas as pl
from jax.experimental.pallas import tpu as pltpu

F32 = jnp.float32
BF16 = jnp.bfloat16

EPS = 1e-6
D_MODEL = 1024
D_FF = 4 * D_MODEL

GLA_HEADS = 4
GLA_DK = 512
GLA_DV = 1024
GLA_HEAD_K = 128
GLA_HEAD_V = 256
GLA_GATE_RANK = 16
GLA_GATE_NORM = 16.0
GLA_BLOCK = 256
GLA_LEVELS = 8
GLA_STEP_HEADS = 2

FOX_HEADS = 16
FOX_HEAD_DIM = 64
FOX_PAIRS = FOX_HEADS // 2
FOX_ONES_ROWS = 16
FOX_VT_ROWS = FOX_HEAD_DIM + FOX_ONES_ROWS

LANES = 128
VMEM_LIMIT = 56 * 1024 * 1024


def _cparams(sem):
    return pltpu.CompilerParams(dimension_semantics=sem, vmem_limit_bytes=VMEM_LIMIT)


def _dot(a, b):
    return jnp.dot(a, b, preferred_element_type=F32)


def _dot_nt(a, b):
    return lax.dot_general(a, b, (((1,), (1,)), ((), ())), preferred_element_type=F32)


def _dot_tn(a, b):
    return lax.dot_general(a, b, (((0,), (0,)), ((), ())), preferred_element_type=F32)


def _rms(x, g):
    ms = jnp.mean(x * x, axis=-1, keepdims=True)
    return x * lax.rsqrt(ms + EPS) * g


def _log_sigmoid(z):
    return jnp.minimum(z, 0.0) - jnp.log(1.0 + jnp.exp(-jnp.abs(z)))


def _sigmoid(z):
    return 0.5 * jnp.tanh(0.5 * z) + 0.5


def _norm_mm_kernel(x_ref, g_ref, w_ref, ws_ref, o_ref, side_ref, *, tn):
    xn = _rms(x_ref[...], g_ref[...]).astype(BF16)
    side_ref[...] = _dot(xn, ws_ref[...])
    for c in range(w_ref.shape[1] // tn):
        cols = slice(c * tn, (c + 1) * tn)
        o_ref[:, cols] = _dot(xn, w_ref[:, cols]).astype(o_ref.dtype)


def _norm_matmul(x, g, w, w_side, tm, tn):
    t, d = x.shape
    n = w.shape[1]
    ns = w_side.shape[1]
    return pl.pallas_call(
        functools.partial(_norm_mm_kernel, tn=tn),
        name="norm_matmul",
        grid=(t // tm,),
        in_specs=[
            pl.BlockSpec((tm, d), lambda i: (i, 0)),
            pl.BlockSpec((1, d), lambda i: (0, 0)),
            pl.BlockSpec((d, n), lambda i: (0, 0)),
            pl.BlockSpec((d, ns), lambda i: (0, 0)),
        ],
        out_specs=[
            pl.BlockSpec((tm, n), lambda i: (i, 0)),
            pl.BlockSpec((tm, ns), lambda i: (i, 0)),
        ],
        out_shape=[jax.ShapeDtypeStruct((t, n), BF16), jax.ShapeDtypeStruct((t, ns), F32)],
        compiler_params=_cparams(("parallel",)),
    )(x, g.reshape(1, d), w, w_side)


def _mix_out_mlp_kernel(a_ref, wo_ref, x_ref, g_ref, w1_ref, w2_ref, gf_ref, o_ref, *, tf, final_norm):
    x1 = x_ref[...] + _dot(a_ref[...], wo_ref[...])
    xn = _rms(x1, g_ref[...]).astype(BF16)
    y = x1
    for c in range(w1_ref.shape[1] // tf):
        h = _dot(xn, w1_ref[:, c * tf:(c + 1) * tf])
        h = jnp.square(jnp.maximum(h, 0.0)).astype(BF16)
        y = y + _dot(h, w2_ref[c * tf:(c + 1) * tf, :])
    if final_norm:
        y = _rms(y, gf_ref[...])
    o_ref[...] = y


def _mix_out_mlp(a, w_out, x, g, w1, w2, gf, tm, tf, final_norm):
    t, d = x.shape
    ff = w1.shape[1]
    resident = lambda shape: pl.BlockSpec(shape, lambda i: (0, 0), pipeline_mode=pl.Buffered(1))
    return pl.pallas_call(
        functools.partial(_mix_out_mlp_kernel, tf=tf, final_norm=final_norm),
        name="mix_out_mlp",
        grid=(t // tm,),
        in_specs=[
            pl.BlockSpec((tm, d), lambda i: (i, 0)),
            resident((d, d)),
            pl.BlockSpec((tm, d), lambda i: (i, 0)),
            resident((1, d)),
            resident((d, ff)),
            resident((ff, d)),
            resident((1, d)),
        ],
        out_specs=pl.BlockSpec((tm, d), lambda i: (i, 0)),
        out_shape=jax.ShapeDtypeStruct((t, d), F32),
        compiler_params=_cparams(("parallel",)),
    )(a, w_out, x, g.reshape(1, d), w1, w2, gf.reshape(1, d))


def _gla_level_map(n):
    i = np.arange(n)[:, None]
    j = np.arange(n)[None, :]
    x = i ^ j
    lvl = np.zeros((n, n), np.int32)
    nz = x > 0
    lvl[nz] = np.floor(np.log2(x[nz])).astype(np.int32) + 1
    lvl = np.where(i > j, lvl, 0)
    lvl = np.where(i == j, GLA_LEVELS, lvl)
    return lvl.astype(np.int32)


def _gla_kernel(q_ref, k_ref, v_ref, r_ref, gl_ref, wup_ref, bgk_ref, gon_ref, lvl_ref,
                o_ref, st_ref):
    L = GLA_BLOCK
    HK, HV = GLA_HEAD_K, GLA_HEAD_V
    heads = range(GLA_STEP_HEADS)

    @pl.when(pl.program_id(2) == 0)
    def _():
        st_ref[...] = jnp.zeros_like(st_ref)

    z = _dot(gl_ref[...], wup_ref[...]) + bgk_ref[...]
    g = _log_sigmoid(z) * (1.0 / GLA_GATE_NORM)

    qs = q_ref[...].astype(F32) * (HK ** -0.5)
    k = k_ref[...].astype(F32)
    lvl = lvl_ref[...]
    H2 = L // 2
    SUB = 8
    row = lax.broadcasted_iota(jnp.int32, qs.shape, 0)

    def halves(x, h):
        lo = [x[r:r + h] for r in range(0, L, 2 * h)]
        up = [x[r + h:r + 2 * h] for r in range(0, L, 2 * h)]
        return lo, up

    def interleave(lo, up):
        return jnp.concatenate([piece for pair in zip(lo, up) for piece in pair], axis=0)

    def diag_products(xq, xk):
        return [[_dot_nt(xq[i * H2:(i + 1) * H2, hh * HK:(hh + 1) * HK],
                         xk[i * H2:(i + 1) * H2, hh * HK:(hh + 1) * HK]) for i in range(2)] for hh in heads]

    def level_update(a_diag, t, xs):
        xs = xs.astype(BF16)
        prods = diag_products(xs, xs)
        return [[jnp.where(lvl == t + 1, prods[hh][i], a_diag[hh][i]) for i in range(2)] for hh in heads]

    prods = diag_products(qs.astype(BF16), k.astype(BF16))
    a_diag = [[jnp.where(lvl == GLA_LEVELS, prods[hh][i], 0.0) for i in range(2)] for hh in heads]
    p = g
    tot = g
    a_off = None
    for t in range(GLA_LEVELS):
        h = 1 << t
        if h < SUB:
            upper = (row & h) != 0
            e = jnp.where(upper, p, tot - p)
            a_diag = level_update(a_diag, t, jnp.where(upper, qs, k) * jnp.exp(e))
            t_lo = pltpu.roll(tot, h, 0)
            t_hi = pltpu.roll(tot, L - h, 0)
            p = p + jnp.where(upper, t_lo, 0.0)
            tot = tot + jnp.where(upper, t_lo, t_hi)
        else:
            p_lo, p_up = halves(p, h)
            t_lo, t_up = halves(tot, h)
            q_up = halves(qs, h)[1]
            k_lo = halves(k, h)[0]
            x_lo = [kk * jnp.exp(tl - pl_) for kk, tl, pl_ in zip(k_lo, t_lo, p_lo)]
            x_up = [qq * jnp.exp(pu) for qq, pu in zip(q_up, p_up)]
            if 2 * h < L:
                a_diag = level_update(a_diag, t, interleave(x_lo, x_up))
            else:
                xu, xl = x_up[0].astype(BF16), x_lo[0].astype(BF16)
                a_off = [_dot_nt(xu[:, hh * HK:(hh + 1) * HK], xl[:, hh * HK:(hh + 1) * HK]) for hh in heads]
            t_new = [tl + tu for tl, tu in zip(t_lo, t_up)]
            p = interleave(p_lo, [pu + tl for pu, tl in zip(p_up, t_lo)])
            tot = interleave(t_new, t_new)

    qe = (qs * jnp.exp(p)).astype(BF16)
    kd = (k * jnp.exp(tot - p)).astype(BF16)
    decay = jnp.exp(tot[0:1, :])
    for hh in heads:
        ks = slice(hh * HK, (hh + 1) * HK)
        vs = slice(hh * HV, (hh + 1) * HV)
        st = st_ref[hh]
        v = v_ref[:, vs]
        a_top = a_diag[hh][0].astype(BF16)
        a_bot = jnp.concatenate([a_off[hh], a_diag[hh][1]], axis=1).astype(BF16)
        o_intra = jnp.concatenate([_dot(a_top, v[:H2]), _dot(a_bot, v)], axis=0)
        o = _dot_nt(qe[:, ks], st.astype(BF16)) + o_intra
        st_ref[hh] = st * decay[:, ks] + _dot_tn(v, kd[:, ks])
        r = r_ref[:, vs].astype(F32)
        o_ref[:, vs] = (_rms(o, gon_ref[...]) * (r * _sigmoid(r))).astype(o_ref.dtype)


def _gla(proj, gl, wup, bgk, gon, batch, seq):
    L = GLA_BLOCK
    nblk = seq // L
    t = batch * seq
    lvl = jnp.asarray(_gla_level_map(L // 2))
    wk = GLA_STEP_HEADS * GLA_HEAD_K
    wv = GLA_STEP_HEADS * GLA_HEAD_V
    kb = GLA_DK // wk
    vb = 2 * GLA_DK // wv
    rb = vb + GLA_DV // wv
    rowblk = lambda b, h, s: b * nblk + s
    return pl.pallas_call(
        _gla_kernel,
        name="gla",
        grid=(batch, GLA_HEADS // GLA_STEP_HEADS, nblk),
        in_specs=[
            pl.BlockSpec((L, wk), lambda b, h, s: (rowblk(b, h, s), h)),
            pl.BlockSpec((L, wk), lambda b, h, s: (rowblk(b, h, s), kb + h)),
            pl.BlockSpec((L, wv), lambda b, h, s: (rowblk(b, h, s), vb + h)),
            pl.BlockSpec((L, wv), lambda b, h, s: (rowblk(b, h, s), rb + h)),
            pl.BlockSpec((L, LANES), lambda b, h, s: (rowblk(b, h, s), 0)),
            pl.BlockSpec((LANES, wk), lambda b, h, s: (0, h)),
            pl.BlockSpec((1, wk), lambda b, h, s: (0, h)),
            pl.BlockSpec((1, GLA_HEAD_V), lambda b, h, s: (0, 0)),
            pl.BlockSpec((L // 2, L // 2), lambda b, h, s: (0, 0)),
        ],
        out_specs=pl.BlockSpec((L, wv), lambda b, h, s: (rowblk(b, h, s), h)),
        out_shape=jax.ShapeDtypeStruct((t, GLA_DV), BF16),
        scratch_shapes=[pltpu.VMEM((GLA_STEP_HEADS, GLA_HEAD_V, GLA_HEAD_K), F32)],
        compiler_params=_cparams(("parallel", "parallel", "arbitrary")),
    )(proj, proj, proj, proj, gl, wup, bgk.reshape(1, GLA_DK), gon.reshape(1, GLA_HEAD_V), lvl)


LOG2E = 1.4426950408889634
C_PIECES = 3


def _split3(x):
    hi = x.astype(BF16)
    r = x - hi.astype(F32)
    mid = r.astype(BF16)
    lo = (r - mid.astype(F32)).astype(BF16)
    return hi, mid, lo


def _fox_aug_matrices():
    n = FOX_HEADS * LANES
    pq = np.zeros((LANES, n), np.float32)
    pk = np.zeros((LANES, n), np.float32)
    oq = np.zeros((1, n), np.float32)
    ok = np.zeros((1, n), np.float32)
    for h in range(FOX_HEADS):
        o = h * LANES + (FOX_HEAD_DIM if h % 2 == 0 else 0)
        for t in range(C_PIECES):
            pq[FOX_HEADS * t + h, o + t] = 1.0
            ok[0, o + t] = 1.0
            pk[FOX_HEADS * t + h, o + C_PIECES + t] = -1.0
            oq[0, o + C_PIECES + t] = 1.0
    return pq, pk, oq, ok


def _fox_prep_kernel(q_ref, k_ref, v_ref, fl_ref, bf_ref, gq_ref, gk_ref, bd_ref, tri_ref,
                     pq_ref, pk_ref, oq_ref, ok_ref, qop_ref, kop_ref, vt_ref, carry_ref):
    @pl.when(pl.program_id(1) == 0)
    def _():
        carry_ref[...] = jnp.zeros_like(carry_ref)

    lf = _log_sigmoid(fl_ref[...] + bf_ref[...])
    hi, mid, lo = _split3(lf)
    tri = tri_ref[...]
    c = carry_ref[...] + (_dot(tri, hi) + _dot(tri, mid) + _dot(tri, lo))
    carry_ref[...] = c[c.shape[0] - 1:, :]

    lane = lax.broadcasted_iota(jnp.int32, (1, LANES), 1)
    hi, mid, lo = _split3(jnp.where(lane < FOX_HEADS, c * LOG2E, 0.0))
    cpack = (hi.astype(F32) + pltpu.roll(mid.astype(F32), FOX_HEADS, 1)
             + pltpu.roll(lo.astype(F32), 2 * FOX_HEADS, 1)).astype(BF16)
    aug_q = _dot(cpack, pq_ref[...]) + oq_ref[...]
    aug_k = _dot(cpack, pk_ref[...]) + ok_ref[...]

    bd = bd_ref[...]

    def head_norm(x, gain):
        sq = x * x
        hi = sq.astype(BF16)
        lo = (sq - hi.astype(F32)).astype(BF16)
        ss = _dot(hi, bd) + _dot(lo, bd)
        return x * lax.rsqrt(ss * (1.0 / FOX_HEAD_DIM) + EPS) * gain

    halves = (lane < FOX_HEAD_DIM, lane >= FOX_HEAD_DIM)
    for pair in range(FOX_PAIRS):
        sl = slice(pair * LANES, (pair + 1) * LANES)
        qn = head_norm(q_ref[:, sl].astype(F32), gq_ref[:, sl]) * (FOX_HEAD_DIM ** -0.5 * LOG2E)
        kn = head_norm(k_ref[:, sl].astype(F32), gk_ref[:, sl])
        for e in range(2):
            hs = slice((2 * pair + e) * LANES, (2 * pair + e + 1) * LANES)
            qop_ref[:, hs] = jnp.where(halves[e], qn, aug_q[:, hs]).astype(BF16)
            kop_ref[:, hs] = jnp.where(halves[e], kn, aug_k[:, hs]).astype(BF16)
        vt = v_ref[:, sl].astype(F32).T.astype(BF16)
        ones = jnp.ones((FOX_ONES_ROWS, vt.shape[1]), BF16)
        for e in range(2):
            r0 = (2 * pair + e) * FOX_VT_ROWS
            vt_ref[0, r0:r0 + FOX_HEAD_DIM, :] = vt[e * FOX_HEAD_DIM:(e + 1) * FOX_HEAD_DIM]
            vt_ref[0, r0 + FOX_HEAD_DIM:r0 + FOX_VT_ROWS, :] = ones


def _fox_prep(proj, fl, b_f, g_q, g_k, batch, seq, ts):
    t = batch * seq
    nblk = seq // ts
    d = D_MODEL
    n_op = FOX_HEADS * LANES
    bd = jnp.asarray(np.kron(np.eye(LANES // FOX_HEAD_DIM), np.ones((FOX_HEAD_DIM, FOX_HEAD_DIM))), BF16)
    tri = jnp.asarray(np.tril(np.ones((ts, ts))), BF16)
    pq, pk, oq, ok = _fox_aug_matrices()
    bf = jnp.zeros((1, LANES), F32).at[0, :FOX_HEADS].set(b_f)
    gq = jnp.tile(g_q, FOX_HEADS).reshape(1, d)
    gk = jnp.tile(g_k, FOX_HEADS).reshape(1, d)
    rowblk = lambda b, s: b * nblk + s
    const = lambda b, s: (0, 0)
    return pl.pallas_call(
        _fox_prep_kernel,
        name="fox_prep",
        grid=(batch, nblk),
        in_specs=[
            pl.BlockSpec((ts, d), lambda b, s: (rowblk(b, s), 0)),
            pl.BlockSpec((ts, d), lambda b, s: (rowblk(b, s), 1)),
            pl.BlockSpec((ts, d), lambda b, s: (rowblk(b, s), 2)),
            pl.BlockSpec((ts, LANES), lambda b, s: (rowblk(b, s), 0)),
            pl.BlockSpec((1, LANES), const),
            pl.BlockSpec((1, d), const),
            pl.BlockSpec((1, d), const),
            pl.BlockSpec((LANES, LANES), const),
            pl.BlockSpec((ts, ts), const),
            pl.BlockSpec((LANES, n_op), const),
            pl.BlockSpec((LANES, n_op), const),
            pl.BlockSpec((1, n_op), const),
            pl.BlockSpec((1, n_op), const),
        ],
        out_specs=[
            pl.BlockSpec((ts, n_op), lambda b, s: (rowblk(b, s), 0)),
            pl.BlockSpec((ts, n_op), lambda b, s: (rowblk(b, s), 0)),
            pl.BlockSpec((1, FOX_HEADS * FOX_VT_ROWS, ts), lambda b, s: (b, 0, s)),
        ],
        out_shape=[
            jax.ShapeDtypeStruct((t, n_op), BF16),
            jax.ShapeDtypeStruct((t, n_op), BF16),
            jax.ShapeDtypeStruct((batch, FOX_HEADS * FOX_VT_ROWS, seq), BF16),
        ],
        scratch_shapes=[pltpu.VMEM((1, LANES), F32)],
        compiler_params=_cparams(("parallel", "arbitrary")),
    )(proj, proj, proj, fl, bf, gq, gk, bd, tri,
      jnp.asarray(pq, BF16), jnp.asarray(pk, BF16), jnp.asarray(oq), jnp.asarray(ok))


def _fox_attn_kernel(q_ref, k_ref, vt_ref, gate_ref, o_ref, acc_ref, m_ref, s_ref, *, tq):
    half = FOX_HEAD_DIM
    nq = q_ref.shape[0] // tq

    def q_block(qi):
        q_rows = pl.ds(pl.multiple_of(qi * tq, tq), tq)
        qq = [q_ref[q_rows, e * LANES:(e + 1) * LANES] for e in range(2)]

        acc_ref[...] = jnp.zeros_like(acc_ref)
        m_ref[...] = jnp.full(m_ref.shape, -1e30, F32)

        def put_scores(kj, slot, e):
            start = pl.multiple_of(kj * tq, tq)
            s_ref[slot, e] = _dot_nt(k_ref[pl.ds(start, tq), e * LANES:(e + 1) * LANES], qq[e])

        def block(kj, slot, e, diagonal):
            start = pl.multiple_of(kj * tq, tq)
            vop = vt_ref[0, e * FOX_VT_ROWS:(e + 1) * FOX_VT_ROWS, pl.ds(start, tq)]
            s = s_ref[slot, e]
            if diagonal:
                key = lax.broadcasted_iota(jnp.int32, (tq, tq), 0)
                qry = lax.broadcasted_iota(jnp.int32, (tq, tq), 1)
                s = jnp.where(key <= qry, s, -jnp.inf)
            m_prev = m_ref[e]
            m_new = jnp.maximum(m_prev, jnp.max(s, axis=0, keepdims=True))
            alpha = jnp.exp2(m_prev - m_new)
            p = jnp.exp2(s - m_new).astype(BF16)
            acc_ref[e] = acc_ref[e] * alpha + _dot(vop, p)
            m_ref[e] = m_new

        def step(nxt, cur, diagonal=False):
            for e in range(2):
                if nxt is not None:
                    put_scores(nxt[0], nxt[1], e)
                if cur is not None:
                    block(cur[0], cur[1], e, diagonal)

        step((0, 0), None)

        def quad_body(i, carry):
            for u in range(4):
                step((4 * i + u + 1, (u + 1) % 2), (4 * i + u, u % 2))
            return carry

        lax.fori_loop(0, qi // 4, quad_body, 0)
        done = (qi // 4) * 4

        @pl.when(qi % 4 >= 2)
        def _():
            step((done + 1, 1), (done, 0))
            step((done + 2, 0), (done + 1, 1))

        @pl.when(qi % 2 == 0)
        def _():
            step(None, (qi, 0), True)

        @pl.when(qi % 2 == 1)
        def _():
            step((qi, 1), (qi - 1, 0))
            step(None, (qi, 1), True)

        outs = []
        for e in range(2):
            acc = acc_ref[e]
            denom = acc[half:half + 1]
            outs.append(acc[:half] * (1.0 / denom))
        o2 = jnp.concatenate(outs, axis=0).T
        o_ref[q_rows, :] = (o2 * _sigmoid(gate_ref[q_rows, :].astype(F32))).astype(o_ref.dtype)

    def q_body(qi, carry):
        q_block(qi)
        return carry

    lax.fori_loop(0, nq, q_body, 0)


def _fox_attn(qop, kop, vt, proj, batch, seq, tq):
    t = batch * seq
    gate_blk = 3 * D_MODEL // LANES
    return pl.pallas_call(
        functools.partial(_fox_attn_kernel, tq=tq),
        name="fox_attn",
        grid=(batch, FOX_PAIRS),
        in_specs=[
            pl.BlockSpec((seq, 2 * LANES), lambda b, hp: (b, hp)),
            pl.BlockSpec((seq, 2 * LANES), lambda b, hp: (b, hp)),
            pl.BlockSpec((1, 2 * FOX_VT_ROWS, seq), lambda b, hp: (b, hp, 0)),
            pl.BlockSpec((seq, LANES), lambda b, hp: (b, gate_blk + hp)),
        ],
        out_specs=pl.BlockSpec((seq, LANES), lambda b, hp: (b, hp)),
        out_shape=jax.ShapeDtypeStruct((t, D_MODEL), BF16),
        scratch_shapes=[pltpu.VMEM((2, FOX_VT_ROWS, tq), F32), pltpu.VMEM((2, 1, tq), F32),
                        pltpu.VMEM((2, 2, tq, tq), F32)],
        compiler_params=_cparams(("parallel", "parallel")),
    )(qop, kop, vt, proj)


def kernel(x, l0_norm_mix, l0_w_in, l0_w_gk_up, l0_b_gk, l0_g_onorm, l0_w_out, l0_norm_ffn, l0_w_ff1, l0_w_ff2, l1_norm_mix, l1_w_in, l1_b_f, l1_g_q, l1_g_k, l1_w_out, l1_norm_ffn, l1_w_ff1, l1_w_ff2, final_norm):
    batch, seq, d = x.shape
    t = batch * seq
    xf = x.reshape(t, d)

    n_main = 2 * GLA_DK + 2 * GLA_DV
    w_main = l0_w_in[:, :n_main].astype(BF16)
    w_gate = jnp.zeros((d, LANES), F32).at[:, :GLA_GATE_RANK].set(l0_w_in[:, n_main:]).astype(BF16)
    w_up = jnp.zeros((LANES, GLA_DK), F32).at[:GLA_GATE_RANK].set(l0_w_gk_up)
    proj, gl = _norm_matmul(xf, l0_norm_mix, w_main, w_gate, 512, 1024)
    og = _gla(proj, gl, w_up, l0_b_gk, l0_g_onorm, batch, seq)
    xf = _mix_out_mlp(og, l0_w_out.astype(BF16), xf, l0_norm_ffn, l0_w_ff1.astype(BF16),
                      l0_w_ff2.astype(BF16), final_norm, 512, 1024, False)

    w_qkv = l1_w_in[:, :3 * d]
    w_f = l1_w_in[:, 3 * d:3 * d + FOX_HEADS]
    w_g = l1_w_in[:, 3 * d + FOX_HEADS:]
    w_main = jnp.concatenate([w_qkv, w_g], axis=1).astype(BF16)
    w_fp = jnp.zeros((d, LANES), F32).at[:, :FOX_HEADS].set(w_f).astype(BF16)
    proj, fl = _norm_matmul(xf, l1_norm_mix, w_main, w_fp, 512, 1024)
    qop, kop, vt = _fox_prep(proj, fl, l1_b_f, l1_g_q, l1_g_k, batch, seq, 512)
    oa = _fox_attn(qop, kop, vt, proj, batch, seq, 512)
    xf = _mix_out_mlp(oa, l1_w_out.astype(BF16), xf, l1_norm_ffn, l1_w_ff1.astype(BF16),
                      l1_w_ff2.astype(BF16), final_norm, 512, 1024, True)
    return xf.reshape(batch, seq, d)
```

```python
import functools

import numpy as np
import jax
import jax.numpy as jnp
from jax import lax
from jax.experimental import pallas as pl
from jax.experimental.pallas import tpu as pltpu

F32 = jnp.float32
BF16 = jnp.bfloat16

EPS = 1e-6
D_MODEL = 1024
D_FF = 4 * D_MODEL

GLA_HEADS = 4
GLA_DK = 512
GLA_DV = 1024
GLA_HEAD_K = 128
GLA_HEAD_V = 256
GLA_GATE_RANK = 16
GLA_GATE_NORM = 16.0
GLA_BLOCK = 256
GLA_LEVELS = 8
GLA_STEP_HEADS = 2

FOX_HEADS = 16
FOX_HEAD_DIM = 64
FOX_PAIRS = FOX_HEADS // 2
FOX_ONES_ROWS = 16
FOX_VT_ROWS = FOX_HEAD_DIM + FOX_ONES_ROWS

LANES = 128
VMEM_LIMIT = 56 * 1024 * 1024


def _cparams(sem):
    return pltpu.CompilerParams(dimension_semantics=sem, vmem_limit_bytes=VMEM_LIMIT)


def _dot(a, b):
    return jnp.dot(a, b, preferred_element_type=F32)


def _dot_nt(a, b):
    return lax.dot_general(a, b, (((1,), (1,)), ((), ())), preferred_element_type=F32)


def _dot_tn(a, b):
    return lax.dot_general(a, b, (((0,), (0,)), ((), ())), preferred_element_type=F32)


def _rms(x, g):
    ms = jnp.mean(x * x, axis=-1, keepdims=True)
    return x * lax.rsqrt(ms + EPS) * g


def _log_sigmoid(z):
    return jnp.minimum(z, 0.0) - jnp.log(1.0 + jnp.exp(-jnp.abs(z)))


def _sigmoid(z):
    return 0.5 * jnp.tanh(0.5 * z) + 0.5


def _norm_mm_kernel(x_ref, g_ref, w_ref, ws_ref, o_ref, side_ref, *, tn):
    xn = _rms(x_ref[...], g_ref[...]).astype(BF16)
    side_ref[...] = _dot(xn, ws_ref[...])
    for c in range(w_ref.shape[1] // tn):
        cols = slice(c * tn, (c + 1) * tn)
        o_ref[:, cols] = _dot(xn, w_ref[:, cols]).astype(o_ref.dtype)


def _norm_matmul(x, g, w, w_side, tm, tn):
    t, d = x.shape
    n = w.shape[1]
    ns = w_side.shape[1]
    return pl.pallas_call(
        functools.partial(_norm_mm_kernel, tn=tn),
        name="norm_matmul",
        grid=(t // tm,),
        in_specs=[
            pl.BlockSpec((tm, d), lambda i: (i, 0)),
            pl.BlockSpec((1, d), lambda i: (0, 0)),
            pl.BlockSpec((d, n), lambda i: (0, 0)),
            pl.BlockSpec((d, ns), lambda i: (0, 0)),
        ],
        out_specs=[
            pl.BlockSpec((tm, n), lambda i: (i, 0)),
            pl.BlockSpec((tm, ns), lambda i: (i, 0)),
        ],
        out_shape=[jax.ShapeDtypeStruct((t, n), BF16), jax.ShapeDtypeStruct((t, ns), F32)],
        compiler_params=_cparams(("parallel",)),
    )(x, g.reshape(1, d), w, w_side)


def _mix_out_mlp_kernel(a_ref, wo_ref, x_ref, g_ref, w1_ref, w2_ref, gf_ref, o_ref, *, tf, final_norm):
    x1 = x_ref[...] + _dot(a_ref[...], wo_ref[...])
    xn = _rms(x1, g_ref[...]).astype(BF16)
    y = x1
    for c in range(w1_ref.shape[1] // tf):
        h = _dot(xn, w1_ref[:, c * tf:(c + 1) * tf])
        h = jnp.square(jnp.maximum(h, 0.0)).astype(BF16)
        y = y + _dot(h, w2_ref[c * tf:(c + 1) * tf, :])
    if final_norm:
        y = _rms(y, gf_ref[...])
    o_ref[...] = y


def _mix_out_mlp(a, w_out, x, g, w1, w2, gf, tm, tf, final_norm):
    t, d = x.shape
    ff = w1.shape[1]
    resident = lambda shape: pl.BlockSpec(shape, lambda i: (0, 0), pipeline_mode=pl.Buffered(1))
    return pl.pallas_call(
        functools.partial(_mix_out_mlp_kernel, tf=tf, final_norm=final_norm),
        name="mix_out_mlp",
        grid=(t // tm,),
        in_specs=[
            pl.BlockSpec((tm, d), lambda i: (i, 0)),
            resident((d, d)),
            pl.BlockSpec((tm, d), lambda i: (i, 0)),
            resident((1, d)),
            resident((d, ff)),
            resident((ff, d)),
            resident((1, d)),
        ],
        out_specs=pl.BlockSpec((tm, d), lambda i: (i, 0)),
        out_shape=jax.ShapeDtypeStruct((t, d), F32),
        compiler_params=_cparams(("parallel",)),
    )(a, w_out, x, g.reshape(1, d), w1, w2, gf.reshape(1, d))


def _gla_level_map(n):
    i = np.arange(n)[:, None]
    j = np.arange(n)[None, :]
    x = i ^ j
    lvl = np.zeros((n, n), np.int32)
    nz = x > 0
    lvl[nz] = np.floor(np.log2(x[nz])).astype(np.int32) + 1
    lvl = np.where(i > j, lvl, 0)
    lvl = np.where(i == j, GLA_LEVELS, lvl)
    return lvl.astype(np.int32)


def _gla_kernel(q_ref, k_ref, v_ref, r_ref, gl_ref, wup_ref, bgk_ref, gon_ref, lvl_ref,
                o_ref, st_ref):
    L = GLA_BLOCK
    HK, HV = GLA_HEAD_K, GLA_HEAD_V
    heads = range(GLA_STEP_HEADS)

    @pl.when(pl.program_id(2) == 0)
    def _():
        st_ref[...] = jnp.zeros_like(st_ref)

    z = _dot(gl_ref[...], wup_ref[...]) + bgk_ref[...]
    g = _log_sigmoid(z) * (1.0 / GLA_GATE_NORM)

    qs = q_ref[...].astype(F32) * (HK ** -0.5)
    k = k_ref[...].astype(F32)
    lvl = lvl_ref[...]
    H2 = L // 2
    SUB = 8
    row = lax.broadcasted_iota(jnp.int32, qs.shape, 0)

    def halves(x, h):
        lo = [x[r:r + h] for r in range(0, L, 2 * h)]
        up = [x[r + h:r + 2 * h] for r in range(0, L, 2 * h)]
        return lo, up

    def interleave(lo, up):
        return jnp.concatenate([piece for pair in zip(lo, up) for piece in pair], axis=0)

    def diag_products(xq, xk):
        return [[_dot_nt(xq[i * H2:(i + 1) * H2, hh * HK:(hh + 1) * HK],
                         xk[i * H2:(i + 1) * H2, hh * HK:(hh + 1) * HK]) for i in range(2)] for hh in heads]

    def level_update(a_diag, t, xs):
        xs = xs.astype(BF16)
        prods = diag_products(xs, xs)
        return [[jnp.where(lvl == t + 1, prods[hh][i], a_diag[hh][i]) for i in range(2)] for hh in heads]

    prods = diag_products(qs.astype(BF16), k.astype(BF16))
    a_diag = [[jnp.where(lvl == GLA_LEVELS, prods[hh][i], 0.0) for i in range(2)] for hh in heads]
    p = g
    tot = g
    a_off = None
    for t in range(GLA_LEVELS):
        h = 1 << t
        if h < SUB:
            upper = (row & h) != 0
            e = jnp.where(upper, p, tot - p)
            a_diag = level_update(a_diag, t, jnp.where(upper, qs, k) * jnp.exp(e))
            t_lo = pltpu.roll(tot, h, 0)
            t_hi = pltpu.roll(tot, L - h, 0)
            p = p + jnp.where(upper, t_lo, 0.0)
            tot = tot + jnp.where(upper, t_lo, t_hi)
        else:
            p_lo, p_up = halves(p, h)
            t_lo, t_up = halves(tot, h)
            q_up = halves(qs, h)[1]
            k_lo = halves(k, h)[0]
            x_lo = [kk * jnp.exp(tl - pl_) for kk, tl, pl_ in zip(k_lo, t_lo, p_lo)]
            x_up = [qq * jnp.exp(pu) for qq, pu in zip(q_up, p_up)]
            if 2 * h < L:
                a_diag = level_update(a_diag, t, interleave(x_lo, x_up))
            else:
                xu, xl = x_up[0].astype(BF16), x_lo[0].astype(BF16)
                a_off = [_dot_nt(xu[:, hh * HK:(hh + 1) * HK], xl[:, hh * HK:(hh + 1) * HK]) for hh in heads]
            t_new = [tl + tu for tl, tu in zip(t_lo, t_up)]
            p = interleave(p_lo, [pu + tl for pu, tl in zip(p_up, t_lo)])
            tot = interleave(t_new, t_new)

    qe = (qs * jnp.exp(p)).astype(BF16)
    kd = (k * jnp.exp(tot - p)).astype(BF16)
    decay = jnp.exp(tot[0:1, :])
    for hh in heads:
        ks = slice(hh * HK, (hh + 1) * HK)
        vs = slice(hh * HV, (hh + 1) * HV)
        st = st_ref[hh]
        v = v_ref[:, vs]
        a_top = a_diag[hh][0].astype(BF16)
        a_bot = jnp.concatenate([a_off[hh], a_diag[hh][1]], axis=1).astype(BF16)
        o_intra = jnp.concatenate([_dot(a_top, v[:H2]), _dot(a_bot, v)], axis=0)
        o = _dot_nt(qe[:, ks], st.astype(BF16)) + o_intra
        st_ref[hh] = st * decay[:, ks] + _dot_tn(v, kd[:, ks])
        r = r_ref[:, vs].astype(F32)
        o_ref[:, vs] = (_rms(o, gon_ref[...]) * (r * _sigmoid(r))).astype(o_ref.dtype)


def _gla(proj, gl, wup, bgk, gon, batch, seq):
    L = GLA_BLOCK
    nblk = seq // L
    t = batch * seq
    lvl = jnp.asarray(_gla_level_map(L // 2))
    wk = GLA_STEP_HEADS * GLA_HEAD_K
    wv = GLA_STEP_HEADS * GLA_HEAD_V
    kb = GLA_DK // wk
    vb = 2 * GLA_DK // wv
    rb = vb + GLA_DV // wv
    rowblk = lambda b, h, s: b * nblk + s
    return pl.pallas_call(
        _gla_kernel,
        name="gla",
        grid=(batch, GLA_HEADS // GLA_STEP_HEADS, nblk),
        in_specs=[
            pl.BlockSpec((L, wk), lambda b, h, s: (rowblk(b, h, s), h)),
            pl.BlockSpec((L, wk), lambda b, h, s: (rowblk(b, h, s), kb + h)),
            pl.BlockSpec((L, wv), lambda b, h, s: (rowblk(b, h, s), vb + h)),
            pl.BlockSpec((L, wv), lambda b, h, s: (rowblk(b, h, s), rb + h)),
            pl.BlockSpec((L, LANES), lambda b, h, s: (rowblk(b, h, s), 0)),
            pl.BlockSpec((LANES, wk), lambda b, h, s: (0, h)),
            pl.BlockSpec((1, wk), lambda b, h, s: (0, h)),
            pl.BlockSpec((1, GLA_HEAD_V), lambda b, h, s: (0, 0)),
            pl.BlockSpec((L // 2, L // 2), lambda b, h, s: (0, 0)),
        ],
        out_specs=pl.BlockSpec((L, wv), lambda b, h, s: (rowblk(b, h, s), h)),
        out_shape=jax.ShapeDtypeStruct((t, GLA_DV), BF16),
        scratch_shapes=[pltpu.VMEM((GLA_STEP_HEADS, GLA_HEAD_V, GLA_HEAD_K), F32)],
        compiler_params=_cparams(("parallel", "parallel", "arbitrary")),
    )(proj, proj, proj, proj, gl, wup, bgk.reshape(1, GLA_DK), gon.reshape(1, GLA_HEAD_V), lvl)


LOG2E = 1.4426950408889634
C_PIECES = 3


def _split3(x):
    hi = x.astype(BF16)
    r = x - hi.astype(F32)
    mid = r.astype(BF16)
    lo = (r - mid.astype(F32)).astype(BF16)
    return hi, mid, lo


def _fox_aug_matrices():
    n = FOX_HEADS * LANES
    pq = np.zeros((LANES, n), np.float32)
    pk = np.zeros((LANES, n), np.float32)
    oq = np.zeros((1, n), np.float32)
    ok = np.zeros((1, n), np.float32)
    for h in range(FOX_HEADS):
        o = h * LANES + (FOX_HEAD_DIM if h % 2 == 0 else 0)
        for t in range(C_PIECES):
            pq[FOX_HEADS * t + h, o + t] = 1.0
            ok[0, o + t] = 1.0
            pk[FOX_HEADS * t + h, o + C_PIECES + t] = -1.0
            oq[0, o + C_PIECES + t] = 1.0
    return pq, pk, oq, ok


def _fox_prep_kernel(q_ref, k_ref, v_ref, fl_ref, bf_ref, gq_ref, gk_ref, bd_ref, tri_ref,
                     pq_ref, pk_ref, oq_ref, ok_ref, qop_ref, kop_ref, vt_ref, carry_ref):
    @pl.when(pl.program_id(1) == 0)
    def _():
        carry_ref[...] = jnp.zeros_like(carry_ref)

    lf = _log_sigmoid(fl_ref[...] + bf_ref[...])
    hi, mid, lo = _split3(lf)
    tri = tri_ref[...]
    c = carry_ref[...] + (_dot(tri, hi) + _dot(tri, mid) + _dot(tri, lo))
    carry_ref[...] = c[c.shape[0] - 1:, :]

    lane = lax.broadcasted_iota(jnp.int32, (1, LANES), 1)
    hi, mid, lo = _split3(jnp.where(lane < FOX_HEADS, c * LOG2E, 0.0))
    cpack = (hi.astype(F32) + pltpu.roll(mid.astype(F32), FOX_HEADS, 1)
             + pltpu.roll(lo.astype(F32), 2 * FOX_HEADS, 1)).astype(BF16)
    aug_q = _dot(cpack, pq_ref[...]) + oq_ref[...]
    aug_k = _dot(cpack, pk_ref[...]) + ok_ref[...]

    bd = bd_ref[...]

    def head_norm(x, gain):
        ss = _dot((x * x).astype(BF16), bd)
        return x * lax.rsqrt(ss * (1.0 / FOX_HEAD_DIM) + EPS) * gain

    halves = (lane < FOX_HEAD_DIM, lane >= FOX_HEAD_DIM)
    for pair in range(FOX_PAIRS):
        sl = slice(pair * LANES, (pair + 1) * LANES)
        qn = head_norm(q_ref[:, sl].astype(F32), gq_ref[:, sl]) * (FOX_HEAD_DIM ** -0.5 * LOG2E)
        kn = head_norm(k_ref[:, sl].astype(F32), gk_ref[:, sl])
        for e in range(2):
            hs = slice((2 * pair + e) * LANES, (2 * pair + e + 1) * LANES)
            qop_ref[:, hs] = jnp.where(halves[e], qn, aug_q[:, hs]).astype(BF16)
            kop_ref[:, hs] = jnp.where(halves[e], kn, aug_k[:, hs]).astype(BF16)
        vt = v_ref[:, sl].astype(F32).T.astype(BF16)
        ones = jnp.ones((FOX_ONES_ROWS, vt.shape[1]), BF16)
        for e in range(2):
            r0 = (2 * pair + e) * FOX_VT_ROWS
            vt_ref[0, r0:r0 + FOX_HEAD_DIM, :] = vt[e * FOX_HEAD_DIM:(e + 1) * FOX_HEAD_DIM]
            vt_ref[0, r0 + FOX_HEAD_DIM:r0 + FOX_VT_ROWS, :] = ones


def _fox_prep(proj, fl, b_f, g_q, g_k, batch, seq, ts):
    t = batch * seq
    nblk = seq // ts
    d = D_MODEL
    n_op = FOX_HEADS * LANES
    bd = jnp.asarray(np.kron(np.eye(LANES // FOX_HEAD_DIM), np.ones((FOX_HEAD_DIM, FOX_HEAD_DIM))), BF16)
    tri = jnp.asarray(np.tril(np.ones((ts, ts))), BF16)
    pq, pk, oq, ok = _fox_aug_matrices()
    bf = jnp.zeros((1, LANES), F32).at[0, :FOX_HEADS].set(b_f)
    gq = jnp.tile(g_q, FOX_HEADS).reshape(1, d)
    gk = jnp.tile(g_k, FOX_HEADS).reshape(1, d)
    rowblk = lambda b, s: b * nblk + s
    const = lambda b, s: (0, 0)
    return pl.pallas_call(
        _fox_prep_kernel,
        name="fox_prep",
        grid=(batch, nblk),
        in_specs=[
            pl.BlockSpec((ts, d), lambda b, s: (rowblk(b, s), 0)),
            pl.BlockSpec((ts, d), lambda b, s: (rowblk(b, s), 1)),
            pl.BlockSpec((ts, d), lambda b, s: (rowblk(b, s), 2)),
            pl.BlockSpec((ts, LANES), lambda b, s: (rowblk(b, s), 0)),
            pl.BlockSpec((1, LANES), const),
            pl.BlockSpec((1, d), const),
            pl.BlockSpec((1, d), const),
            pl.BlockSpec((LANES, LANES), const),
            pl.BlockSpec((ts, ts), const),
            pl.BlockSpec((LANES, n_op), const),
            pl.BlockSpec((LANES, n_op), const),
            pl.BlockSpec((1, n_op), const),
            pl.BlockSpec((1, n_op), const),
        ],
        out_specs=[
            pl.BlockSpec((ts, n_op), lambda b, s: (rowblk(b, s), 0)),
            pl.BlockSpec((ts, n_op), lambda b, s: (rowblk(b, s), 0)),
            pl.BlockSpec((1, FOX_HEADS * FOX_VT_ROWS, ts), lambda b, s: (b, 0, s)),
        ],
        out_shape=[
            jax.ShapeDtypeStruct((t, n_op), BF16),
            jax.ShapeDtypeStruct((t, n_op), BF16),
            jax.ShapeDtypeStruct((batch, FOX_HEADS * FOX_VT_ROWS, seq), BF16),
        ],
        scratch_shapes=[pltpu.VMEM((1, LANES), F32)],
        compiler_params=_cparams(("parallel", "arbitrary")),
    )(proj, proj, proj, fl, bf, gq, gk, bd, tri,
      jnp.asarray(pq, BF16), jnp.asarray(pk, BF16), jnp.asarray(oq), jnp.asarray(ok))


def _fox_attn_kernel(q_ref, k_ref, vt_ref, gate_ref, o_ref, acc_ref, m_ref, s_ref, *, tq):
    half = FOX_HEAD_DIM
    nq = q_ref.shape[0] // tq
    hand = (2, 3)

    def q_operands(qi):
        rows = pl.ds(pl.multiple_of(qi * tq, tq), tq)
        return [q_ref[rows, e * LANES:(e + 1) * LANES] for e in range(2)]

    def put_scores(qq, kj, slot, e):
        start = pl.multiple_of(kj * tq, tq)
        s_ref[slot, e] = _dot_nt(k_ref[pl.ds(start, tq), e * LANES:(e + 1) * LANES], qq[e])

    def block(kj, slot, e, diagonal):
        start = pl.multiple_of(kj * tq, tq)
        vop = vt_ref[0, e * FOX_VT_ROWS:(e + 1) * FOX_VT_ROWS, pl.ds(start, tq)]
        s = s_ref[slot, e]
        if diagonal:
            key = lax.broadcasted_iota(jnp.int32, (tq, tq), 0)
            qry = lax.broadcasted_iota(jnp.int32, (tq, tq), 1)
            s = jnp.where(key <= qry, s, -jnp.inf)
        m_prev = m_ref[e]
        m_new = jnp.maximum(m_prev, jnp.max(s, axis=0, keepdims=True))
        alpha = jnp.exp2(m_prev - m_new)
        p = jnp.exp2(s - m_new).astype(BF16)
        acc_ref[e] = acc_ref[e] * alpha + _dot(vop, p)
        m_ref[e] = m_new

    def step(nxt, cur, diagonal=False):
        for e in range(2):
            put_scores(nxt[0], nxt[1], nxt[2], e)
            block(cur[0], cur[1], e, diagonal)

    def q_block(qi, parity):
        q_rows = pl.ds(pl.multiple_of(qi * tq, tq), tq)
        qq = q_operands(qi)
        following = (q_operands(jnp.minimum(qi + 1, nq - 1)), 0, hand[1 - parity])

        acc_ref[...] = jnp.zeros_like(acc_ref)
        m_ref[...] = jnp.full(m_ref.shape, -1e30, F32)

        def middle(j0, count):
            for u in range(count):
                step((qq, j0 + u + 1, u % 2), (j0 + u, (u + 1) % 2))

        def general():
            step((qq, 1, 1), (0, hand[parity]))
            n_mid = qi - 1

            def quad_body(i, carry):
                middle(1 + 4 * i, 4)
                return carry

            lax.fori_loop(0, n_mid // 4, quad_body, 0)
            j0 = 1 + (n_mid // 4) * 4
            for rem in range(1 - parity, 4, 2):
                if rem:
                    pl.when(n_mid % 4 == rem)(functools.partial(middle, j0, rem))
            step(following, (qi, parity), True)

        if parity == 0:
            pl.when(qi == 0)(lambda: step(following, (0, hand[0]), True))
            pl.when(qi > 0)(general)
        else:
            general()

        outs = []
        for e in range(2):
            acc = acc_ref[e]
            denom = acc[half:half + 1]
            outs.append(acc[:half] * (1.0 / denom))
        o2 = jnp.concatenate(outs, axis=0).T
        o_ref[q_rows, :] = (o2 * _sigmoid(gate_ref[q_rows, :].astype(F32))).astype(o_ref.dtype)

    first = q_operands(0)
    for e in range(2):
        put_scores(first, 0, hand[0], e)

    def pair_body(jp, carry):
        q_block(2 * jp, 0)
        q_block(2 * jp + 1, 1)
        return carry

    lax.fori_loop(0, nq // 2, pair_body, 0)


def _fox_attn(qop, kop, vt, proj, batch, seq, tq):
    t = batch * seq
    gate_blk = 3 * D_MODEL // LANES
    return pl.pallas_call(
        functools.partial(_fox_attn_kernel, tq=tq),
        name="fox_attn",
        grid=(batch, FOX_PAIRS),
        in_specs=[
            pl.BlockSpec((seq, 2 * LANES), lambda b, hp: (b, hp)),
            pl.BlockSpec((seq, 2 * LANES), lambda b, hp: (b, hp)),
            pl.BlockSpec((1, 2 * FOX_VT_ROWS, seq), lambda b, hp: (b, hp, 0)),
            pl.BlockSpec((seq, LANES), lambda b, hp: (b, gate_blk + hp)),
        ],
        out_specs=pl.BlockSpec((seq, LANES), lambda b, hp: (b, hp)),
        out_shape=jax.ShapeDtypeStruct((t, D_MODEL), BF16),
        scratch_shapes=[pltpu.VMEM((2, FOX_VT_ROWS, tq), F32), pltpu.VMEM((2, 1, tq), F32),
                        pltpu.VMEM((4, 2, tq, tq), F32)],
        compiler_params=_cparams(("parallel", "parallel")),
    )(qop, kop, vt, proj)


def kernel(x, l0_norm_mix, l0_w_in, l0_w_gk_up, l0_b_gk, l0_g_onorm, l0_w_out, l0_norm_ffn, l0_w_ff1, l0_w_ff2, l1_norm_mix, l1_w_in, l1_b_f, l1_g_q, l1_g_k, l1_w_out, l1_norm_ffn, l1_w_ff1, l1_w_ff2, final_norm):
    batch, seq, d = x.shape
    t = batch * seq
    xf = x.reshape(t, d)

    n_main = 2 * GLA_DK + 2 * GLA_DV
    w_main = l0_w_in[:, :n_main].astype(BF16)
    w_gate = jnp.zeros((d, LANES), F32).at[:, :GLA_GATE_RANK].set(l0_w_in[:, n_main:]).astype(BF16)
    w_up = jnp.zeros((LANES, GLA_DK), F32).at[:GLA_GATE_RANK].set(l0_w_gk_up)
    proj, gl = _norm_matmul(xf, l0_norm_mix, w_main, w_gate, 512, 1024)
    og = _gla(proj, gl, w_up, l0_b_gk, l0_g_onorm, batch, seq)
    xf = _mix_out_mlp(og, l0_w_out.astype(BF16), xf, l0_norm_ffn, l0_w_ff1.astype(BF16),
                      l0_w_ff2.astype(BF16), final_norm, 512, 1024, False)

    w_qkv = l1_w_in[:, :3 * d]
    w_f = l1_w_in[:, 3 * d:3 * d + FOX_HEADS]
    w_g = l1_w_in[:, 3 * d + FOX_HEADS:]
    w_main = jnp.concatenate([w_qkv, w_g], axis=1).astype(BF16)
    w_fp = jnp.zeros((d, LANES), F32).at[:, :FOX_HEADS].set(w_f).astype(BF16)
    proj, fl = _norm_matmul(xf, l1_norm_mix, w_main, w_fp, 512, 1024)
    qop, kop, vt = _fox_prep(proj, fl, l1_b_f, l1_g_q, l1_g_k, batch, seq, 512)
    oa = _fox_attn(qop, kop, vt, proj, batch, seq, 512)
    xf = _mix_out_mlp(oa, l1_w_out.astype(BF16), xf, l1_norm_ffn, l1_w_ff1.astype(BF16),
                      l1_w_ff2.astype(BF16), final_norm, 512, 1024, True)
    return xf.reshape(batch, seq, d)
```

```python
import functools

import numpy as np
import jax
import jax.numpy as jnp
from jax import lax
from jax.experimental import pallas as pl
from jax.experimental.pallas import tpu as pltpu

F32 = jnp.float32
BF16 = jnp.bfloat16

EPS = 1e-6
D_MODEL = 1024
D_FF = 4 * D_MODEL

GLA_HEADS = 4
GLA_DK = 512
GLA_DV = 1024
GLA_HEAD_K = 128
GLA_HEAD_V = 256
GLA_GATE_RANK = 16
GLA_GATE_NORM = 16.0
GLA_BLOCK = 256
GLA_LEVELS = 8
GLA_STEP_HEADS = 2

FOX_HEADS = 16
FOX_HEAD_DIM = 64
FOX_PAIRS = FOX_HEADS // 2
FOX_ONES_ROWS = 16
FOX_VT_ROWS = FOX_HEAD_DIM + FOX_ONES_ROWS

LANES = 128
VMEM_LIMIT = 56 * 1024 * 1024


def _cparams(sem):
    return pltpu.CompilerParams(dimension_semantics=sem, vmem_limit_bytes=VMEM_LIMIT)


def _dot(a, b):
    return jnp.dot(a, b, preferred_element_type=F32)


def _dot_nt(a, b):
    return lax.dot_general(a, b, (((1,), (1,)), ((), ())), preferred_element_type=F32)


def _dot_tn(a, b):
    return lax.dot_general(a, b, (((0,), (0,)), ((), ())), preferred_element_type=F32)


def _rms(x, g):
    ms = jnp.mean(x * x, axis=-1, keepdims=True)
    return x * lax.rsqrt(ms + EPS) * g


def _log_sigmoid(z):
    return jnp.minimum(z, 0.0) - jnp.log(1.0 + jnp.exp(-jnp.abs(z)))


def _sigmoid(z):
    return 0.5 * jnp.tanh(0.5 * z) + 0.5


def _mix_out_mlp_kernel(a_ref, wo_ref, x_ref, g_ref, w1_ref, w2_ref, gf_ref, o_ref, *, tf, final_norm):
    x1 = x_ref[...] + _dot(a_ref[...], wo_ref[...])
    xn = _rms(x1, g_ref[...]).astype(BF16)
    y = x1
    for c in range(w1_ref.shape[1] // tf):
        h = _dot(xn, w1_ref[:, c * tf:(c + 1) * tf])
        h = jnp.square(jnp.maximum(h, 0.0)).astype(BF16)
        y = y + _dot(h, w2_ref[c * tf:(c + 1) * tf, :])
    if final_norm:
        y = _rms(y, gf_ref[...])
    o_ref[...] = y


def _mix_out_mlp(a, w_out, x, g, w1, w2, gf, tm, tf, final_norm):
    t, d = x.shape
    ff = w1.shape[1]
    resident = lambda shape: pl.BlockSpec(shape, lambda i: (0, 0), pipeline_mode=pl.Buffered(1))
    return pl.pallas_call(
        functools.partial(_mix_out_mlp_kernel, tf=tf, final_norm=final_norm),
        name="mix_out_mlp",
        grid=(t // tm,),
        in_specs=[
            pl.BlockSpec((tm, d), lambda i: (i, 0)),
            resident((d, d)),
            pl.BlockSpec((tm, d), lambda i: (i, 0)),
            resident((1, d)),
            resident((d, ff)),
            resident((ff, d)),
            resident((1, d)),
        ],
        out_specs=pl.BlockSpec((tm, d), lambda i: (i, 0)),
        out_shape=jax.ShapeDtypeStruct((t, d), F32),
        compiler_params=_cparams(("parallel",)),
    )(a, w_out, x, g.reshape(1, d), w1, w2, gf.reshape(1, d))


def _gla_level_map(n):
    i = np.arange(n)[:, None]
    j = np.arange(n)[None, :]
    x = i ^ j
    lvl = np.zeros((n, n), np.int32)
    nz = x > 0
    lvl[nz] = np.floor(np.log2(x[nz])).astype(np.int32) + 1
    lvl = np.where(i > j, lvl, 0)
    lvl = np.where(i == j, GLA_LEVELS, lvl)
    return lvl.astype(np.int32)


def _gla_block(q, k, v, r, gl, wup, bgk, gon, lvl, st_ref, st_base):
    L = GLA_BLOCK
    HK, HV = GLA_HEAD_K, GLA_HEAD_V
    heads = range(GLA_STEP_HEADS)

    z = _dot(gl, wup) + bgk
    g = _log_sigmoid(z) * (1.0 / GLA_GATE_NORM)

    qs = q * (HK ** -0.5)
    H2 = L // 2
    SUB = 8
    row = lax.broadcasted_iota(jnp.int32, qs.shape, 0)

    def halves(x, h):
        lo = [x[r:r + h] for r in range(0, L, 2 * h)]
        up = [x[r + h:r + 2 * h] for r in range(0, L, 2 * h)]
        return lo, up

    def interleave(lo, up):
        return jnp.concatenate([piece for pair in zip(lo, up) for piece in pair], axis=0)

    def diag_products(xq, xk):
        return [[_dot_nt(xq[i * H2:(i + 1) * H2, hh * HK:(hh + 1) * HK],
                         xk[i * H2:(i + 1) * H2, hh * HK:(hh + 1) * HK]) for i in range(2)] for hh in heads]

    def level_update(a_diag, t, xs):
        xs = xs.astype(BF16)
        prods = diag_products(xs, xs)
        return [[jnp.where(lvl == t + 1, prods[hh][i], a_diag[hh][i]) for i in range(2)] for hh in heads]

    prods = diag_products(qs.astype(BF16), k.astype(BF16))
    a_diag = [[jnp.where(lvl == GLA_LEVELS, prods[hh][i], 0.0) for i in range(2)] for hh in heads]
    p = g
    tot = g
    a_off = None
    for t in range(GLA_LEVELS):
        h = 1 << t
        if h < SUB:
            upper = (row & h) != 0
            e = jnp.where(upper, p, tot - p)
            a_diag = level_update(a_diag, t, jnp.where(upper, qs, k) * jnp.exp(e))
            t_lo = pltpu.roll(tot, h, 0)
            t_hi = pltpu.roll(tot, L - h, 0)
            p = p + jnp.where(upper, t_lo, 0.0)
            tot = tot + jnp.where(upper, t_lo, t_hi)
        else:
            p_lo, p_up = halves(p, h)
            t_lo, t_up = halves(tot, h)
            q_up = halves(qs, h)[1]
            k_lo = halves(k, h)[0]
            x_lo = [kk * jnp.exp(tl - pl_) for kk, tl, pl_ in zip(k_lo, t_lo, p_lo)]
            x_up = [qq * jnp.exp(pu) for qq, pu in zip(q_up, p_up)]
            if 2 * h < L:
                a_diag = level_update(a_diag, t, interleave(x_lo, x_up))
            else:
                xu, xl = x_up[0].astype(BF16), x_lo[0].astype(BF16)
                a_off = [_dot_nt(xu[:, hh * HK:(hh + 1) * HK], xl[:, hh * HK:(hh + 1) * HK]) for hh in heads]
            t_new = [tl + tu for tl, tu in zip(t_lo, t_up)]
            p = interleave(p_lo, [pu + tl for pu, tl in zip(p_up, t_lo)])
            tot = interleave(t_new, t_new)

    qe = (qs * jnp.exp(p)).astype(BF16)
    kd = (k * jnp.exp(tot - p)).astype(BF16)
    decay = jnp.exp(tot[0:1, :])
    outs = []
    for hh in heads:
        ks = slice(hh * HK, (hh + 1) * HK)
        vs = slice(hh * HV, (hh + 1) * HV)
        st = st_ref[st_base + hh]
        vh = v[:, vs]
        a_top = a_diag[hh][0].astype(BF16)
        a_bot = jnp.concatenate([a_off[hh], a_diag[hh][1]], axis=1).astype(BF16)
        o_intra = jnp.concatenate([_dot(a_top, vh[:H2]), _dot(a_bot, vh)], axis=0)
        o = _dot_nt(qe[:, ks], st.astype(BF16)) + o_intra
        st_ref[st_base + hh] = st * decay[:, ks] + _dot_tn(vh, kd[:, ks])
        rh = r[:, vs]
        outs.append((_rms(o, gon) * (rh * _sigmoid(rh))).astype(BF16))
    return jnp.concatenate(outs, axis=1)


def _gla_proj_kernel(x_ref, g_ref, w_ref, wg_ref, wup_ref, bgk_ref, gon_ref, lvl_ref, o_ref, st_ref):
    L = GLA_BLOCK
    wk = GLA_STEP_HEADS * GLA_HEAD_K
    wv = GLA_STEP_HEADS * GLA_HEAD_V

    @pl.when(pl.program_id(1) == 0)
    def _():
        st_ref[...] = jnp.zeros_like(st_ref)

    xn = _rms(x_ref[...], g_ref[...]).astype(BF16)
    q = _dot(xn, w_ref[:, 0:GLA_DK])
    k = _dot(xn, w_ref[:, GLA_DK:2 * GLA_DK])
    gl = _dot(xn, wg_ref[...])
    v = _dot(xn, w_ref[:, 2 * GLA_DK:2 * GLA_DK + GLA_DV]).astype(BF16)
    r = _dot(xn, w_ref[:, 2 * GLA_DK + GLA_DV:])
    gon = gon_ref[...]
    lvl = lvl_ref[...]
    for blk in range(x_ref.shape[0] // L):
        rows = slice(blk * L, (blk + 1) * L)
        for hp in range(GLA_HEADS // GLA_STEP_HEADS):
            ks = slice(hp * wk, (hp + 1) * wk)
            vs = slice(hp * wv, (hp + 1) * wv)
            o_ref[rows, vs] = _gla_block(q[rows, ks], k[rows, ks], v[rows, vs], r[rows, vs], gl[rows],
                                         wup_ref[:, ks], bgk_ref[:, ks], gon, lvl, st_ref, hp * GLA_STEP_HEADS)


def _gla_proj(x, g, w, w_gate, wup, bgk, gon, batch, seq, ts):
    t, d = x.shape
    nblk = seq // ts
    lvl = jnp.asarray(_gla_level_map(GLA_BLOCK // 2))
    const = lambda b, s: (0, 0)
    resident = lambda shape: pl.BlockSpec(shape, const, pipeline_mode=pl.Buffered(1))
    return pl.pallas_call(
        _gla_proj_kernel,
        name="gla_proj",
        grid=(batch, nblk),
        in_specs=[
            pl.BlockSpec((ts, d), lambda b, s: (b * nblk + s, 0)),
            resident((1, d)),
            resident(w.shape),
            resident((d, LANES)),
            resident((LANES, GLA_DK)),
            resident((1, GLA_DK)),
            resident((1, GLA_HEAD_V)),
            resident((GLA_BLOCK // 2, GLA_BLOCK // 2)),
        ],
        out_specs=pl.BlockSpec((ts, GLA_DV), lambda b, s: (b * nblk + s, 0)),
        out_shape=jax.ShapeDtypeStruct((t, GLA_DV), BF16),
        scratch_shapes=[pltpu.VMEM((GLA_HEADS, GLA_HEAD_V, GLA_HEAD_K), F32)],
        compiler_params=_cparams(("parallel", "arbitrary")),
    )(x, g.reshape(1, d), w, w_gate, wup, bgk.reshape(1, GLA_DK), gon.reshape(1, GLA_HEAD_V), lvl)


LOG2E = 1.4426950408889634
C_PIECES = 3


def _split3(x):
    hi = x.astype(BF16)
    r = x - hi.astype(F32)
    mid = r.astype(BF16)
    lo = (r - mid.astype(F32)).astype(BF16)
    return hi, mid, lo


def _fox_aug_matrices():
    n = FOX_HEADS * LANES
    pq = np.zeros((LANES, n), np.float32)
    pk = np.zeros((LANES, n), np.float32)
    oq = np.zeros((1, n), np.float32)
    ok = np.zeros((1, n), np.float32)
    for h in range(FOX_HEADS):
        o = h * LANES + (FOX_HEAD_DIM if h % 2 == 0 else 0)
        for t in range(C_PIECES):
            pq[FOX_HEADS * t + h, o + t] = 1.0
            ok[0, o + t] = 1.0
            pk[FOX_HEADS * t + h, o + C_PIECES + t] = -1.0
            oq[0, o + C_PIECES + t] = 1.0
    return pq, pk, oq, ok


def _fox_proj_prep_kernel(x_ref, g_ref, w_ref, wf_ref, bf_ref, gq_ref, gk_ref, bd_ref, tri_ref,
                          pq_ref, pk_ref, oq_ref, ok_ref, qop_ref, kop_ref, vt_ref, gate_ref, carry_ref):
    d = D_MODEL

    @pl.when(pl.program_id(1) == 0)
    def _():
        carry_ref[...] = jnp.zeros_like(carry_ref)

    xn = _rms(x_ref[...], g_ref[...]).astype(BF16)
    q = _dot(xn, w_ref[:, 0:d])
    k = _dot(xn, w_ref[:, d:2 * d])

    lf = _log_sigmoid(_dot(xn, wf_ref[...]) + bf_ref[...])
    hi, mid, lo = _split3(lf)
    tri = tri_ref[...]
    c_hm = _dot(tri, jnp.concatenate([hi, mid], axis=1))
    c = carry_ref[...] + (c_hm[:, :LANES] + c_hm[:, LANES:] + _dot(tri, lo))
    carry_ref[...] = c[c.shape[0] - 1:, :]

    lane = lax.broadcasted_iota(jnp.int32, (1, LANES), 1)
    hi, mid, lo = _split3(jnp.where(lane < FOX_HEADS, c * LOG2E, 0.0))
    cpack = (hi.astype(F32) + pltpu.roll(mid.astype(F32), FOX_HEADS, 1)
             + pltpu.roll(lo.astype(F32), 2 * FOX_HEADS, 1)).astype(BF16)
    aug_q = _dot(cpack, pq_ref[...]) + oq_ref[...]
    aug_k = _dot(cpack, pk_ref[...]) + ok_ref[...]

    v = _dot(xn, w_ref[:, 2 * d:3 * d])
    gate_ref[...] = _dot(xn, w_ref[:, 3 * d:4 * d]).astype(gate_ref.dtype)

    bd = bd_ref[...]

    def head_norm(x, gain):
        ss = _dot((x * x).astype(BF16), bd)
        return x * lax.rsqrt(ss * (1.0 / FOX_HEAD_DIM) + EPS) * gain

    halves = (lane < FOX_HEAD_DIM, lane >= FOX_HEAD_DIM)
    for pair in range(FOX_PAIRS):
        sl = slice(pair * LANES, (pair + 1) * LANES)
        if pair % 2 == 0:
            sl2 = slice(pair * LANES, (pair + 2) * LANES)
            qn2 = head_norm(q[:, sl2], gq_ref[:, sl2]) * (FOX_HEAD_DIM ** -0.5 * LOG2E)
            kn2 = head_norm(k[:, sl2], gk_ref[:, sl2])
        inner = slice((pair % 2) * LANES, (pair % 2 + 1) * LANES)
        qn, kn = qn2[:, inner], kn2[:, inner]
        for e in range(2):
            hs = slice((2 * pair + e) * LANES, (2 * pair + e + 1) * LANES)
            qop_ref[:, hs] = jnp.where(halves[e], qn, aug_q[:, hs]).astype(BF16)
            kop_ref[:, hs] = jnp.where(halves[e], kn, aug_k[:, hs]).astype(BF16)
        vt = v[:, sl].T.astype(BF16)
        ones = jnp.ones((FOX_ONES_ROWS, vt.shape[1]), BF16)
        for e in range(2):
            r0 = (2 * pair + e) * FOX_VT_ROWS
            vt_ref[0, r0:r0 + FOX_HEAD_DIM, :] = vt[e * FOX_HEAD_DIM:(e + 1) * FOX_HEAD_DIM]
            vt_ref[0, r0 + FOX_HEAD_DIM:r0 + FOX_VT_ROWS, :] = ones


def _fox_proj_prep(x, g, w, w_f, b_f, g_q, g_k, batch, seq, ts):
    t, d = x.shape
    nblk = seq // ts
    n_op = FOX_HEADS * LANES
    bd = jnp.asarray(np.kron(np.eye(2 * LANES // FOX_HEAD_DIM), np.ones((FOX_HEAD_DIM, FOX_HEAD_DIM))), BF16)
    tri = jnp.asarray(np.tril(np.ones((ts, ts))), BF16)
    pq, pk, oq, ok = _fox_aug_matrices()
    bf = jnp.zeros((1, LANES), F32).at[0, :FOX_HEADS].set(b_f)
    gq = jnp.tile(g_q, FOX_HEADS).reshape(1, d)
    gk = jnp.tile(g_k, FOX_HEADS).reshape(1, d)
    rows = lambda b, s: (b * nblk + s, 0)
    const = lambda b, s: (0, 0)
    resident = lambda shape: pl.BlockSpec(shape, const, pipeline_mode=pl.Buffered(1))
    return pl.pallas_call(
        _fox_proj_prep_kernel,
        name="fox_proj_prep",
        grid=(batch, nblk),
        in_specs=[
            pl.BlockSpec((ts, d), rows),
            resident((1, d)),
            resident((d, 4 * d)),
            resident((d, LANES)),
            resident((1, LANES)),
            resident((1, d)),
            resident((1, d)),
            resident((2 * LANES, 2 * LANES)),
            resident((ts, ts)),
            resident((LANES, n_op)),
            resident((LANES, n_op)),
            resident((1, n_op)),
            resident((1, n_op)),
        ],
        out_specs=[
            pl.BlockSpec((ts, n_op), rows),
            pl.BlockSpec((ts, n_op), rows),
            pl.BlockSpec((1, FOX_HEADS * FOX_VT_ROWS, ts), lambda b, s: (b, 0, s)),
            pl.BlockSpec((ts, d), rows),
        ],
        out_shape=[
            jax.ShapeDtypeStruct((t, n_op), BF16),
            jax.ShapeDtypeStruct((t, n_op), BF16),
            jax.ShapeDtypeStruct((batch, FOX_HEADS * FOX_VT_ROWS, seq), BF16),
            jax.ShapeDtypeStruct((t, d), BF16),
        ],
        scratch_shapes=[pltpu.VMEM((1, LANES), F32)],
        compiler_params=_cparams(("parallel", "arbitrary")),
    )(x, g.reshape(1, d), w, w_f, bf, gq, gk, bd, tri,
      jnp.asarray(pq, BF16), jnp.asarray(pk, BF16), jnp.asarray(oq), jnp.asarray(ok))


def _fox_attn_kernel(q_ref, k_ref, vt_ref, gate_ref, o_ref, acc_ref, m_ref, s_ref, *, tq):
    half = FOX_HEAD_DIM
    nq = q_ref.shape[0] // tq
    hand = (2, 3)

    def q_operands(qi):
        rows = pl.ds(pl.multiple_of(qi * tq, tq), tq)
        return [q_ref[rows, e * LANES:(e + 1) * LANES] for e in range(2)]

    def put_scores(qq, kj, slot, e):
        start = pl.multiple_of(kj * tq, tq)
        s_ref[slot, e] = _dot_nt(k_ref[pl.ds(start, tq), e * LANES:(e + 1) * LANES], qq[e])

    def block(kj, slot, e, diagonal):
        start = pl.multiple_of(kj * tq, tq)
        vop = vt_ref[0, e * FOX_VT_ROWS:(e + 1) * FOX_VT_ROWS, pl.ds(start, tq)]
        s = s_ref[slot, e]
        if diagonal:
            key = lax.broadcasted_iota(jnp.int32, (tq, tq), 0)
            qry = lax.broadcasted_iota(jnp.int32, (tq, tq), 1)
            s = jnp.where(key <= qry, s, -jnp.inf)
        m_prev = m_ref[e]
        m_new = jnp.maximum(m_prev, jnp.max(s, axis=0, keepdims=True))
        alpha = jnp.exp2(m_prev - m_new)
        p = jnp.exp2(s - m_new).astype(BF16)
        acc_ref[e] = acc_ref[e] * alpha + _dot(vop, p)
        m_ref[e] = m_new

    def step(nxt, cur, diagonal=False):
        for e in range(2):
            put_scores(nxt[0], nxt[1], nxt[2], e)
            block(cur[0], cur[1], e, diagonal)

    def q_block(qi, parity):
        q_rows = pl.ds(pl.multiple_of(qi * tq, tq), tq)
        qq = q_operands(qi)
        following = (q_operands(jnp.minimum(qi + 1, nq - 1)), 0, hand[1 - parity])

        acc_ref[...] = jnp.zeros_like(acc_ref)
        m_ref[...] = jnp.full(m_ref.shape, -1e30, F32)

        def middle(j0, count):
            for u in range(count):
                step((qq, j0 + u + 1, u % 2), (j0 + u, (u + 1) % 2))

        def general():
            step((qq, 1, 1), (0, hand[parity]))
            n_mid = qi - 1

            def quad_body(i, carry):
                middle(1 + 4 * i, 4)
                return carry

            lax.fori_loop(0, n_mid // 4, quad_body, 0)
            j0 = 1 + (n_mid // 4) * 4
            for rem in range(1 - parity, 4, 2):
                if rem:
                    pl.when(n_mid % 4 == rem)(functools.partial(middle, j0, rem))
            step(following, (qi, parity), True)

        if parity == 0:
            pl.when(qi == 0)(lambda: step(following, (0, hand[0]), True))
            pl.when(qi > 0)(general)
        else:
            general()

        outs = []
        for e in range(2):
            acc = acc_ref[e]
            denom = acc[half:half + 1]
            outs.append(acc[:half] * (1.0 / denom))
        o2 = jnp.concatenate(outs, axis=0).T
        o_ref[q_rows, :] = (o2 * _sigmoid(gate_ref[q_rows, :].astype(F32))).astype(o_ref.dtype)

    first = q_operands(0)
    for e in range(2):
        put_scores(first, 0, hand[0], e)

    def pair_body(jp, carry):
        q_block(2 * jp, 0)
        q_block(2 * jp + 1, 1)
        return carry

    lax.fori_loop(0, nq // 2, pair_body, 0)


def _fox_attn(qop, kop, vt, gate, batch, seq, tq):
    t = batch * seq
    return pl.pallas_call(
        functools.partial(_fox_attn_kernel, tq=tq),
        name="fox_attn",
        grid=(batch, FOX_PAIRS),
        in_specs=[
            pl.BlockSpec((seq, 2 * LANES), lambda b, hp: (b, hp)),
            pl.BlockSpec((seq, 2 * LANES), lambda b, hp: (b, hp)),
            pl.BlockSpec((1, 2 * FOX_VT_ROWS, seq), lambda b, hp: (b, hp, 0)),
            pl.BlockSpec((seq, LANES), lambda b, hp: (b, hp)),
        ],
        out_specs=pl.BlockSpec((seq, LANES), lambda b, hp: (b, hp)),
        out_shape=jax.ShapeDtypeStruct((t, D_MODEL), BF16),
        scratch_shapes=[pltpu.VMEM((2, FOX_VT_ROWS, tq), F32), pltpu.VMEM((2, 1, tq), F32),
                        pltpu.VMEM((4, 2, tq, tq), F32)],
        compiler_params=_cparams(("parallel", "parallel")),
    )(qop, kop, vt, gate)


def kernel(x, l0_norm_mix, l0_w_in, l0_w_gk_up, l0_b_gk, l0_g_onorm, l0_w_out, l0_norm_ffn, l0_w_ff1, l0_w_ff2, l1_norm_mix, l1_w_in, l1_b_f, l1_g_q, l1_g_k, l1_w_out, l1_norm_ffn, l1_w_ff1, l1_w_ff2, final_norm):
    batch, seq, d = x.shape
    t = batch * seq
    xf = x.reshape(t, d)

    n_main = 2 * GLA_DK + 2 * GLA_DV
    w_main = l0_w_in[:, :n_main].astype(BF16)
    w_gate = jnp.zeros((d, LANES), F32).at[:, :GLA_GATE_RANK].set(l0_w_in[:, n_main:]).astype(BF16)
    w_up = jnp.zeros((LANES, GLA_DK), F32).at[:GLA_GATE_RANK].set(l0_w_gk_up)
    og = _gla_proj(xf, l0_norm_mix, w_main, w_gate, w_up, l0_b_gk, l0_g_onorm, batch, seq, 512)
    xf = _mix_out_mlp(og, l0_w_out.astype(BF16), xf, l0_norm_ffn, l0_w_ff1.astype(BF16),
                      l0_w_ff2.astype(BF16), final_norm, 512, 1024, False)

    w_qkv = l1_w_in[:, :3 * d]
    w_f = l1_w_in[:, 3 * d:3 * d + FOX_HEADS]
    w_g = l1_w_in[:, 3 * d + FOX_HEADS:]
    w_main = jnp.concatenate([w_qkv, w_g], axis=1).astype(BF16)
    w_fp = jnp.zeros((d, LANES), F32).at[:, :FOX_HEADS].set(w_f).astype(BF16)
    qop, kop, vt, gate = _fox_proj_prep(xf, l1_norm_mix, w_main, w_fp, l1_b_f, l1_g_q, l1_g_k, batch, seq, 512)
    oa = _fox_attn(qop, kop, vt, gate, batch, seq, 512)
    xf = _mix_out_mlp(oa, l1_w_out.astype(BF16), xf, l1_norm_ffn, l1_w_ff1.astype(BF16),
                      l1_w_ff2.astype(BF16), final_norm, 512, 1024, True)
    return xf.reshape(batch, seq, d)
```

```python
import functools

import numpy as np
import jax
import jax.numpy as jnp
from jax import lax
from jax.experimental import pallas as pl
from jax.experimental.pallas import tpu as pltpu

F32 = jnp.float32
BF16 = jnp.bfloat16

EPS = 1e-6
D_MODEL = 1024
D_FF = 4 * D_MODEL

GLA_HEADS = 4
GLA_DK = 512
GLA_DV = 1024
GLA_HEAD_K = 128
GLA_HEAD_V = 256
GLA_GATE_RANK = 16
GLA_GATE_NORM = 16.0
GLA_BLOCK = 256
GLA_LEVELS = 8
GLA_STEP_HEADS = 2

FOX_HEADS = 16
FOX_HEAD_DIM = 64
FOX_PAIRS = FOX_HEADS // 2
FOX_ONES_ROWS = 16
FOX_VT_ROWS = FOX_HEAD_DIM + FOX_ONES_ROWS

LANES = 128
VMEM_LIMIT = 56 * 1024 * 1024


def _cparams(sem):
    return pltpu.CompilerParams(dimension_semantics=sem, vmem_limit_bytes=VMEM_LIMIT)


def _dot(a, b):
    return jnp.dot(a, b, preferred_element_type=F32)


def _dot_nt(a, b):
    return lax.dot_general(a, b, (((1,), (1,)), ((), ())), preferred_element_type=F32)


def _dot_tn(a, b):
    return lax.dot_general(a, b, (((0,), (0,)), ((), ())), preferred_element_type=F32)


def _rms(x, g):
    ms = jnp.mean(x * x, axis=-1, keepdims=True)
    return x * lax.rsqrt(ms + EPS) * g


def _log_sigmoid(z):
    return jnp.minimum(z, 0.0) - jnp.log(1.0 + jnp.exp(-jnp.abs(z)))


def _sigmoid(z):
    return 0.5 * jnp.tanh(0.5 * z) + 0.5


def _mix_out_mlp_kernel(a_ref, wo_ref, x_ref, g_ref, w1_ref, w2_ref, gf_ref, o_ref, *, tf, final_norm):
    x1 = x_ref[...] + _dot(a_ref[...], wo_ref[...])
    xn = _rms(x1, g_ref[...]).astype(BF16)
    y = x1
    for c in range(w1_ref.shape[1] // tf):
        h = _dot(xn, w1_ref[:, c * tf:(c + 1) * tf])
        h = jnp.square(jnp.maximum(h, 0.0)).astype(BF16)
        y = y + _dot(h, w2_ref[c * tf:(c + 1) * tf, :])
    if final_norm:
        y = _rms(y, gf_ref[...])
    o_ref[...] = y


def _mix_out_mlp(a, w_out, x, g, w1, w2, gf, tm, tf, final_norm):
    t, d = x.shape
    ff = w1.shape[1]
    resident = lambda shape: pl.BlockSpec(shape, lambda i: (0, 0), pipeline_mode=pl.Buffered(1))
    return pl.pallas_call(
        functools.partial(_mix_out_mlp_kernel, tf=tf, final_norm=final_norm),
        name="mix_out_mlp",
        grid=(t // tm,),
        in_specs=[
            pl.BlockSpec((tm, d), lambda i: (i, 0)),
            resident((d, d)),
            pl.BlockSpec((tm, d), lambda i: (i, 0)),
            resident((1, d)),
            resident((d, ff)),
            resident((ff, d)),
            resident((1, d)),
        ],
        out_specs=pl.BlockSpec((tm, d), lambda i: (i, 0)),
        out_shape=jax.ShapeDtypeStruct((t, d), F32),
        compiler_params=_cparams(("parallel",)),
    )(a, w_out, x, g.reshape(1, d), w1, w2, gf.reshape(1, d))


def _gla_level_map(n):
    i = np.arange(n)[:, None]
    j = np.arange(n)[None, :]
    x = i ^ j
    lvl = np.zeros((n, n), np.int32)
    nz = x > 0
    lvl[nz] = np.floor(np.log2(x[nz])).astype(np.int32) + 1
    lvl = np.where(i > j, lvl, 0)
    lvl = np.where(i == j, GLA_LEVELS, lvl)
    return lvl.astype(np.int32)


def _gla_block(q, k, v, r, gl, wup, bgk, gon, lvl, st_ref, st_base):
    L = GLA_BLOCK
    HK, HV = GLA_HEAD_K, GLA_HEAD_V
    heads = range(GLA_STEP_HEADS)

    z = _dot(gl, wup) + bgk
    g = _log_sigmoid(z) * (1.0 / GLA_GATE_NORM)

    qs = q * (HK ** -0.5)
    H2 = L // 2
    SUB = 8
    row = lax.broadcasted_iota(jnp.int32, qs.shape, 0)

    def halves(x, h):
        lo = [x[r:r + h] for r in range(0, L, 2 * h)]
        up = [x[r + h:r + 2 * h] for r in range(0, L, 2 * h)]
        return lo, up

    def interleave(lo, up):
        return jnp.concatenate([piece for pair in zip(lo, up) for piece in pair], axis=0)

    def diag_products(xq, xk):
        return [[_dot_nt(xq[i * H2:(i + 1) * H2, hh * HK:(hh + 1) * HK],
                         xk[i * H2:(i + 1) * H2, hh * HK:(hh + 1) * HK]) for i in range(2)] for hh in heads]

    def level_update(a_diag, t, xs):
        xs = xs.astype(BF16)
        prods = diag_products(xs, xs)
        return [[jnp.where(lvl == t + 1, prods[hh][i], a_diag[hh][i]) for i in range(2)] for hh in heads]

    prods = diag_products(qs.astype(BF16), k.astype(BF16))
    a_diag = [[jnp.where(lvl == GLA_LEVELS, prods[hh][i], 0.0) for i in range(2)] for hh in heads]
    p = g
    tot = g
    a_off = None
    for t in range(GLA_LEVELS):
        h = 1 << t
        if h < SUB:
            upper = (row & h) != 0
            e = jnp.where(upper, p, tot - p)
            a_diag = level_update(a_diag, t, jnp.where(upper, qs, k) * jnp.exp(e))
            t_lo = pltpu.roll(tot, h, 0)
            t_hi = pltpu.roll(tot, L - h, 0)
            p = p + jnp.where(upper, t_lo, 0.0)
            tot = tot + jnp.where(upper, t_lo, t_hi)
        else:
            p_lo, p_up = halves(p, h)
            t_lo, t_up = halves(tot, h)
            q_up = halves(qs, h)[1]
            k_lo = halves(k, h)[0]
            x_lo = [kk * jnp.exp(tl - pl_) for kk, tl, pl_ in zip(k_lo, t_lo, p_lo)]
            x_up = [qq * jnp.exp(pu) for qq, pu in zip(q_up, p_up)]
            if 2 * h < L:
                a_diag = level_update(a_diag, t, interleave(x_lo, x_up))
            else:
                xu, xl = x_up[0].astype(BF16), x_lo[0].astype(BF16)
                a_off = [_dot_nt(xu[:, hh * HK:(hh + 1) * HK], xl[:, hh * HK:(hh + 1) * HK]) for hh in heads]
            t_new = [tl + tu for tl, tu in zip(t_lo, t_up)]
            p = interleave(p_lo, [pu + tl for pu, tl in zip(p_up, t_lo)])
            tot = interleave(t_new, t_new)

    qe = (qs * jnp.exp(p)).astype(BF16)
    kd = (k * jnp.exp(tot - p)).astype(BF16)
    decay = jnp.exp(tot[0:1, :])
    outs = []
    for hh in heads:
        ks = slice(hh * HK, (hh + 1) * HK)
        vs = slice(hh * HV, (hh + 1) * HV)
        st = st_ref[st_base + hh]
        vh = v[:, vs]
        a_top = a_diag[hh][0].astype(BF16)
        a_bot = jnp.concatenate([a_off[hh], a_diag[hh][1]], axis=1).astype(BF16)
        o_intra = jnp.concatenate([_dot(a_top, vh[:H2]), _dot(a_bot, vh)], axis=0)
        o = _dot_nt(qe[:, ks], st.astype(BF16)) + o_intra
        st_ref[st_base + hh] = st * decay[:, ks] + _dot_tn(vh, kd[:, ks])
        rh = r[:, vs]
        outs.append((_rms(o, gon) * (rh * _sigmoid(rh))).astype(BF16))
    return jnp.concatenate(outs, axis=1)


def _gla_proj_kernel(x_ref, g_ref, w_ref, wg_ref, wup_ref, bgk_ref, gon_ref, lvl_ref, o_ref, st_ref):
    L = GLA_BLOCK
    wk = GLA_STEP_HEADS * GLA_HEAD_K
    wv = GLA_STEP_HEADS * GLA_HEAD_V

    @pl.when(pl.program_id(1) == 0)
    def _():
        st_ref[...] = jnp.zeros_like(st_ref)

    xn = _rms(x_ref[...], g_ref[...]).astype(BF16)
    q = _dot(xn, w_ref[:, 0:GLA_DK])
    k = _dot(xn, w_ref[:, GLA_DK:2 * GLA_DK])
    gl = _dot(xn, wg_ref[...])
    v = _dot(xn, w_ref[:, 2 * GLA_DK:2 * GLA_DK + GLA_DV]).astype(BF16)
    r = _dot(xn, w_ref[:, 2 * GLA_DK + GLA_DV:])
    gon = gon_ref[...]
    lvl = lvl_ref[...]
    for blk in range(x_ref.shape[0] // L):
        rows = slice(blk * L, (blk + 1) * L)
        for hp in range(GLA_HEADS // GLA_STEP_HEADS):
            ks = slice(hp * wk, (hp + 1) * wk)
            vs = slice(hp * wv, (hp + 1) * wv)
            o_ref[rows, vs] = _gla_block(q[rows, ks], k[rows, ks], v[rows, vs], r[rows, vs], gl[rows],
                                         wup_ref[:, ks], bgk_ref[:, ks], gon, lvl, st_ref, hp * GLA_STEP_HEADS)


def _gla_proj(x, g, w, w_gate, wup, bgk, gon, batch, seq, ts):
    t, d = x.shape
    nblk = seq // ts
    lvl = jnp.asarray(_gla_level_map(GLA_BLOCK // 2))
    const = lambda b, s: (0, 0)
    resident = lambda shape: pl.BlockSpec(shape, const, pipeline_mode=pl.Buffered(1))
    return pl.pallas_call(
        _gla_proj_kernel,
        name="gla_proj",
        grid=(batch, nblk),
        in_specs=[
            pl.BlockSpec((ts, d), lambda b, s: (b * nblk + s, 0)),
            resident((1, d)),
            resident(w.shape),
            resident((d, LANES)),
            resident((LANES, GLA_DK)),
            resident((1, GLA_DK)),
            resident((1, GLA_HEAD_V)),
            resident((GLA_BLOCK // 2, GLA_BLOCK // 2)),
        ],
        out_specs=pl.BlockSpec((ts, GLA_DV), lambda b, s: (b * nblk + s, 0)),
        out_shape=jax.ShapeDtypeStruct((t, GLA_DV), BF16),
        scratch_shapes=[pltpu.VMEM((GLA_HEADS, GLA_HEAD_V, GLA_HEAD_K), F32)],
        compiler_params=_cparams(("parallel", "arbitrary")),
    )(x, g.reshape(1, d), w, w_gate, wup, bgk.reshape(1, GLA_DK), gon.reshape(1, GLA_HEAD_V), lvl)


LOG2E = 1.4426950408889634
C_PIECES = 3


def _split3(x):
    hi = x.astype(BF16)
    r = x - hi.astype(F32)
    mid = r.astype(BF16)
    lo = (r - mid.astype(F32)).astype(BF16)
    return hi, mid, lo


def _fox_aug_matrices():
    n = FOX_HEADS * LANES
    pq = np.zeros((LANES, n), np.float32)
    pk = np.zeros((LANES, n), np.float32)
    oq = np.zeros((1, n), np.float32)
    ok = np.zeros((1, n), np.float32)
    for h in range(FOX_HEADS):
        o = h * LANES + (FOX_HEAD_DIM if h % 2 == 0 else 0)
        for t in range(C_PIECES):
            pq[FOX_HEADS * t + h, o + t] = 1.0
            ok[0, o + t] = 1.0
            pk[FOX_HEADS * t + h, o + C_PIECES + t] = -1.0
            oq[0, o + C_PIECES + t] = 1.0
    return pq, pk, oq, ok


def _fox_proj_prep_kernel(x_ref, g_ref, w_ref, wf_ref, bf_ref, gq_ref, gk_ref, bd_ref, tri_ref,
                          pq_ref, pk_ref, oq_ref, ok_ref, qop_ref, kop_ref, vt_ref, gate_ref, carry_ref):
    d = D_MODEL

    @pl.when(pl.program_id(1) == 0)
    def _():
        carry_ref[...] = jnp.zeros_like(carry_ref)

    xn = _rms(x_ref[...], g_ref[...]).astype(BF16)
    q = _dot(xn, w_ref[:, 0:d])
    k = _dot(xn, w_ref[:, d:2 * d])

    lf = _log_sigmoid(_dot(xn, wf_ref[...]) + bf_ref[...])
    hi, mid, lo = _split3(lf)
    tri = tri_ref[...]
    c_hm = _dot(tri, jnp.concatenate([hi, mid], axis=1))
    c = carry_ref[...] + (c_hm[:, :LANES] + c_hm[:, LANES:] + _dot(tri, lo))
    carry_ref[...] = c[c.shape[0] - 1:, :]

    lane = lax.broadcasted_iota(jnp.int32, (1, LANES), 1)
    hi, mid, lo = _split3(jnp.where(lane < FOX_HEADS, c * LOG2E, 0.0))
    cpack = (hi.astype(F32) + pltpu.roll(mid.astype(F32), FOX_HEADS, 1)
             + pltpu.roll(lo.astype(F32), 2 * FOX_HEADS, 1)).astype(BF16)
    aug_q = _dot(cpack, pq_ref[...]) + oq_ref[...]
    aug_k = _dot(cpack, pk_ref[...]) + ok_ref[...]

    v = _dot(xn, w_ref[:, 2 * d:3 * d])
    gate_ref[...] = _dot(xn, w_ref[:, 3 * d:4 * d]).astype(gate_ref.dtype)

    bd = bd_ref[...]

    def head_norm(x, gain):
        ss = _dot((x * x).astype(BF16), bd)
        return x * lax.rsqrt(ss * (1.0 / FOX_HEAD_DIM) + EPS) * gain

    halves = (lane < FOX_HEAD_DIM, lane >= FOX_HEAD_DIM)
    for pair in range(FOX_PAIRS):
        sl = slice(pair * LANES, (pair + 1) * LANES)
        if pair % 2 == 0:
            sl2 = slice(pair * LANES, (pair + 2) * LANES)
            qn2 = head_norm(q[:, sl2], gq_ref[:, sl2]) * (FOX_HEAD_DIM ** -0.5 * LOG2E)
            kn2 = head_norm(k[:, sl2], gk_ref[:, sl2])
        inner = slice((pair % 2) * LANES, (pair % 2 + 1) * LANES)
        qn, kn = qn2[:, inner], kn2[:, inner]
        for e in range(2):
            hs = slice((2 * pair + e) * LANES, (2 * pair + e + 1) * LANES)
            qop_ref[:, hs] = jnp.where(halves[e], qn, aug_q[:, hs]).astype(BF16)
            kop_ref[:, hs] = jnp.where(halves[e], kn, aug_k[:, hs]).astype(BF16)
        vt = v[:, sl].T.astype(BF16)
        ones = jnp.ones((FOX_ONES_ROWS, vt.shape[1]), BF16)
        for e in range(2):
            r0 = (2 * pair + e) * FOX_VT_ROWS
            vt_ref[0, r0:r0 + FOX_HEAD_DIM, :] = vt[e * FOX_HEAD_DIM:(e + 1) * FOX_HEAD_DIM]
            vt_ref[0, r0 + FOX_HEAD_DIM:r0 + FOX_VT_ROWS, :] = ones


def _fox_proj_prep(x, g, w, w_f, b_f, g_q, g_k, batch, seq, ts):
    t, d = x.shape
    nblk = seq // ts
    n_op = FOX_HEADS * LANES
    bd = jnp.asarray(np.kron(np.eye(2 * LANES // FOX_HEAD_DIM), np.ones((FOX_HEAD_DIM, FOX_HEAD_DIM))), BF16)
    tri = jnp.asarray(np.tril(np.ones((ts, ts))), BF16)
    pq, pk, oq, ok = _fox_aug_matrices()
    bf = jnp.zeros((1, LANES), F32).at[0, :FOX_HEADS].set(b_f)
    gq = jnp.tile(g_q, FOX_HEADS).reshape(1, d)
    gk = jnp.tile(g_k, FOX_HEADS).reshape(1, d)
    rows = lambda b, s: (b * nblk + s, 0)
    const = lambda b, s: (0, 0)
    resident = lambda shape: pl.BlockSpec(shape, const, pipeline_mode=pl.Buffered(1))
    return pl.pallas_call(
        _fox_proj_prep_kernel,
        name="fox_proj_prep",
        grid=(batch, nblk),
        in_specs=[
            pl.BlockSpec((ts, d), rows),
            resident((1, d)),
            resident((d, 4 * d)),
            resident((d, LANES)),
            resident((1, LANES)),
            resident((1, d)),
            resident((1, d)),
            resident((2 * LANES, 2 * LANES)),
            resident((ts, ts)),
            resident((LANES, n_op)),
            resident((LANES, n_op)),
            resident((1, n_op)),
            resident((1, n_op)),
        ],
        out_specs=[
            pl.BlockSpec((ts, n_op), rows),
            pl.BlockSpec((ts, n_op), rows),
            pl.BlockSpec((1, FOX_HEADS * FOX_VT_ROWS, ts), lambda b, s: (b, 0, s)),
            pl.BlockSpec((ts, d), rows),
        ],
        out_shape=[
            jax.ShapeDtypeStruct((t, n_op), BF16),
            jax.ShapeDtypeStruct((t, n_op), BF16),
            jax.ShapeDtypeStruct((batch, FOX_HEADS * FOX_VT_ROWS, seq), BF16),
            jax.ShapeDtypeStruct((t, d), BF16),
        ],
        scratch_shapes=[pltpu.VMEM((1, LANES), F32)],
        compiler_params=_cparams(("parallel", "arbitrary")),
    )(x, g.reshape(1, d), w, w_f, bf, gq, gk, bd, tri,
      jnp.asarray(pq, BF16), jnp.asarray(pk, BF16), jnp.asarray(oq), jnp.asarray(ok))


def _fox_attn_kernel(q_ref, k_ref, vt_ref, gate_ref, o_ref, acc_ref, m_ref, s_ref, *, tq):
    half = FOX_HEAD_DIM
    nq = q_ref.shape[2] // tq
    hand = (2, 3)

    def q_operands(qi):
        cols = pl.ds(pl.multiple_of(qi * tq, tq), tq)
        return [q_ref[0, e * LANES:(e + 1) * LANES, cols] for e in range(2)]

    def put_scores(qq, kj, slot, e):
        start = pl.multiple_of(kj * tq, tq)
        s_ref[slot, e] = _dot(k_ref[pl.ds(start, tq), e * LANES:(e + 1) * LANES], qq[e])

    def block(kj, slot, e, diagonal):
        start = pl.multiple_of(kj * tq, tq)
        vop = vt_ref[0, e * FOX_VT_ROWS:(e + 1) * FOX_VT_ROWS, pl.ds(start, tq)]
        s = s_ref[slot, e]
        if diagonal:
            key = lax.broadcasted_iota(jnp.int32, (tq, tq), 0)
            qry = lax.broadcasted_iota(jnp.int32, (tq, tq), 1)
            s = jnp.where(key <= qry, s, -jnp.inf)
        m_prev = m_ref[e]
        m_new = jnp.maximum(m_prev, jnp.max(s, axis=0, keepdims=True))
        alpha = jnp.exp2(m_prev - m_new)
        p = jnp.exp2(s - m_new).astype(BF16)
        acc_ref[e] = acc_ref[e] * alpha + _dot(vop, p)
        m_ref[e] = m_new

    def step(nxt, cur, diagonal=False):
        for e in range(2):
            put_scores(nxt[0], nxt[1], nxt[2], e)
            block(cur[0], cur[1], e, diagonal)

    def q_block(qi, parity):
        q_rows = pl.ds(pl.multiple_of(qi * tq, tq), tq)
        qq = q_operands(qi)
        following = (q_operands(jnp.minimum(qi + 1, nq - 1)), 0, hand[1 - parity])

        acc_ref[...] = jnp.zeros_like(acc_ref)
        m_ref[...] = jnp.full(m_ref.shape, -1e30, F32)

        def middle(j0, count):
            for u in range(count):
                step((qq, j0 + u + 1, u % 2), (j0 + u, (u + 1) % 2))

        def general():
            step((qq, 1, 1), (0, hand[parity]))
            n_mid = qi - 1

            def quad_body(i, carry):
                middle(1 + 4 * i, 4)
                return carry

            lax.fori_loop(0, n_mid // 4, quad_body, 0)
            j0 = 1 + (n_mid // 4) * 4
            for rem in range(1 - parity, 4, 2):
                if rem:
                    pl.when(n_mid % 4 == rem)(functools.partial(middle, j0, rem))
            step(following, (qi, parity), True)

        if parity == 0:
            pl.when(qi == 0)(lambda: step(following, (0, hand[0]), True))
            pl.when(qi > 0)(general)
        else:
            general()

        outs = []
        for e in range(2):
            acc = acc_ref[e]
            denom = acc[half:half + 1]
            outs.append(acc[:half] * (1.0 / denom))
        o2 = jnp.concatenate(outs, axis=0).T
        o_ref[q_rows, :] = (o2 * _sigmoid(gate_ref[q_rows, :].astype(F32))).astype(o_ref.dtype)

    first = q_operands(0)
    for e in range(2):
        put_scores(first, 0, hand[0], e)

    def pair_body(jp, carry):
        q_block(2 * jp, 0)
        q_block(2 * jp + 1, 1)
        return carry

    lax.fori_loop(0, nq // 2, pair_body, 0)


def _fox_attn(qop, kop, vt, gate, batch, seq, tq):
    t = batch * seq
    return pl.pallas_call(
        functools.partial(_fox_attn_kernel, tq=tq),
        name="fox_attn",
        grid=(batch, FOX_PAIRS),
        in_specs=[
            pl.BlockSpec((1, 2 * LANES, seq), lambda b, hp: (b, hp, 0)),
            pl.BlockSpec((seq, 2 * LANES), lambda b, hp: (b, hp)),
            pl.BlockSpec((1, 2 * FOX_VT_ROWS, seq), lambda b, hp: (b, hp, 0)),
            pl.BlockSpec((seq, LANES), lambda b, hp: (b, hp)),
        ],
        out_specs=pl.BlockSpec((seq, LANES), lambda b, hp: (b, hp)),
        out_shape=jax.ShapeDtypeStruct((t, D_MODEL), BF16),
        scratch_shapes=[pltpu.VMEM((2, FOX_VT_ROWS, tq), F32), pltpu.VMEM((2, 1, tq), F32),
                        pltpu.VMEM((4, 2, tq, tq), F32)],
        compiler_params=_cparams(("parallel", "parallel")),
    )(qop, kop, vt, gate)


def kernel(x, l0_norm_mix, l0_w_in, l0_w_gk_up, l0_b_gk, l0_g_onorm, l0_w_out, l0_norm_ffn, l0_w_ff1, l0_w_ff2, l1_norm_mix, l1_w_in, l1_b_f, l1_g_q, l1_g_k, l1_w_out, l1_norm_ffn, l1_w_ff1, l1_w_ff2, final_norm):
    batch, seq, d = x.shape
    t = batch * seq
    xf = x.reshape(t, d)

    n_main = 2 * GLA_DK + 2 * GLA_DV
    w_main = l0_w_in[:, :n_main].astype(BF16)
    w_gate = jnp.zeros((d, LANES), F32).at[:, :GLA_GATE_RANK].set(l0_w_in[:, n_main:]).astype(BF16)
    w_up = jnp.zeros((LANES, GLA_DK), F32).at[:GLA_GATE_RANK].set(l0_w_gk_up)
    og = _gla_proj(xf, l0_norm_mix, w_main, w_gate, w_up, l0_b_gk, l0_g_onorm, batch, seq, 512)
    xf = _mix_out_mlp(og, l0_w_out.astype(BF16), xf, l0_norm_ffn, l0_w_ff1.astype(BF16),
                      l0_w_ff2.astype(BF16), final_norm, 512, 1024, False)

    w_qkv = l1_w_in[:, :3 * d]
    w_f = l1_w_in[:, 3 * d:3 * d + FOX_HEADS]
    w_g = l1_w_in[:, 3 * d + FOX_HEADS:]
    w_main = jnp.concatenate([w_qkv, w_g], axis=1).astype(BF16)
    w_fp = jnp.zeros((d, LANES), F32).at[:, :FOX_HEADS].set(w_f).astype(BF16)
    qop, kop, vt, gate = _fox_proj_prep(xf, l1_norm_mix, w_main, w_fp, l1_b_f, l1_g_q, l1_g_k, batch, seq, 512)
    qop_t = qop.reshape(batch, seq, FOX_HEADS * LANES).transpose(0, 2, 1)
    oa = _fox_attn(qop_t, kop, vt, gate, batch, seq, 512)
    xf = _mix_out_mlp(oa, l1_w_out.astype(BF16), xf, l1_norm_ffn, l1_w_ff1.astype(BF16),
                      l1_w_ff2.astype(BF16), final_norm, 512, 1024, True)
    return xf.reshape(batch, seq, d)
```

```python
import functools

import numpy as np
import jax
import jax.numpy as jnp
from jax import lax
from jax.experimental import pallas as pl
from jax.experimental.pallas import tpu as pltpu

F32 = jnp.float32
BF16 = jnp.bfloat16

EPS = 1e-6
D_MODEL = 1024
D_FF = 4 * D_MODEL

GLA_HEADS = 4
GLA_DK = 512
GLA_DV = 1024
GLA_HEAD_K = 128
GLA_HEAD_V = 256
GLA_GATE_RANK = 16
GLA_GATE_NORM = 16.0
GLA_BLOCK = 256
GLA_LEVELS = 8
GLA_STEP_HEADS = 2

FOX_HEADS = 16
FOX_HEAD_DIM = 64
FOX_PAIRS = FOX_HEADS // 2
FOX_ONES_ROWS = 16
FOX_VT_ROWS = FOX_HEAD_DIM + FOX_ONES_ROWS

LANES = 128
VMEM_LIMIT = 56 * 1024 * 1024


def _cparams(sem):
    return pltpu.CompilerParams(dimension_semantics=sem, vmem_limit_bytes=VMEM_LIMIT)


def _dot(a, b):
    return jnp.dot(a, b, preferred_element_type=F32)


def _dot_nt(a, b):
    return lax.dot_general(a, b, (((1,), (1,)), ((), ())), preferred_element_type=F32)


def _dot_tn(a, b):
    return lax.dot_general(a, b, (((0,), (0,)), ((), ())), preferred_element_type=F32)


def _rms(x, g):
    ms = jnp.mean(x * x, axis=-1, keepdims=True)
    return x * lax.rsqrt(ms + EPS) * g


def _log_sigmoid(z):
    return jnp.minimum(z, 0.0) - jnp.log(1.0 + jnp.exp(-jnp.abs(z)))


def _sigmoid(z):
    return 0.5 * jnp.tanh(0.5 * z) + 0.5


def _mix_out_mlp_kernel(a_ref, wo_ref, x_ref, g_ref, w1_ref, w2_ref, gf_ref, o_ref, *, tf, final_norm):
    x1 = x_ref[...] + _dot(a_ref[...], wo_ref[...])
    xn = _rms(x1, g_ref[...]).astype(BF16)
    y = x1
    for c in range(w1_ref.shape[1] // tf):
        h = _dot(xn, w1_ref[:, c * tf:(c + 1) * tf])
        h = jnp.square(jnp.maximum(h, 0.0)).astype(BF16)
        y = y + _dot(h, w2_ref[c * tf:(c + 1) * tf, :])
    if final_norm:
        y = _rms(y, gf_ref[...])
    o_ref[...] = y


def _mix_out_mlp(a, w_out, x, g, w1, w2, gf, tm, tf, final_norm):
    t, d = x.shape
    ff = w1.shape[1]
    resident = lambda shape: pl.BlockSpec(shape, lambda i: (0, 0), pipeline_mode=pl.Buffered(1))
    return pl.pallas_call(
        functools.partial(_mix_out_mlp_kernel, tf=tf, final_norm=final_norm),
        name="mix_out_mlp",
        grid=(t // tm,),
        in_specs=[
            pl.BlockSpec((tm, d), lambda i: (i, 0)),
            resident((d, d)),
            pl.BlockSpec((tm, d), lambda i: (i, 0)),
            resident((1, d)),
            resident((d, ff)),
            resident((ff, d)),
            resident((1, d)),
        ],
        out_specs=pl.BlockSpec((tm, d), lambda i: (i, 0)),
        out_shape=jax.ShapeDtypeStruct((t, d), F32),
        compiler_params=_cparams(("parallel",)),
    )(a, w_out, x, g.reshape(1, d), w1, w2, gf.reshape(1, d))


def _gla_level_map(n):
    i = np.arange(n)[:, None]
    j = np.arange(n)[None, :]
    x = i ^ j
    lvl = np.zeros((n, n), np.int32)
    nz = x > 0
    lvl[nz] = np.floor(np.log2(x[nz])).astype(np.int32) + 1
    lvl = np.where(i > j, lvl, 0)
    lvl = np.where(i == j, GLA_LEVELS, lvl)
    return lvl.astype(np.int32)


def _gla_block(q, k, v, r, gl, wup, bgk, gon, lvl, st_ref, st_base):
    L = GLA_BLOCK
    HK, HV = GLA_HEAD_K, GLA_HEAD_V
    heads = range(GLA_STEP_HEADS)

    z = _dot(gl, wup) + bgk
    g = _log_sigmoid(z) * (1.0 / GLA_GATE_NORM)

    qs = q * (HK ** -0.5)
    H2 = L // 2
    SUB = 8
    row = lax.broadcasted_iota(jnp.int32, qs.shape, 0)

    def halves(x, h):
        lo = [x[r:r + h] for r in range(0, L, 2 * h)]
        up = [x[r + h:r + 2 * h] for r in range(0, L, 2 * h)]
        return lo, up

    def interleave(lo, up):
        return jnp.concatenate([piece for pair in zip(lo, up) for piece in pair], axis=0)

    def diag_products(xq, xk):
        return [[_dot_nt(xq[i * H2:(i + 1) * H2, hh * HK:(hh + 1) * HK],
                         xk[i * H2:(i + 1) * H2, hh * HK:(hh + 1) * HK]) for i in range(2)] for hh in heads]

    def level_update(a_diag, t, xs):
        xs = xs.astype(BF16)
        prods = diag_products(xs, xs)
        return [[jnp.where(lvl == t + 1, prods[hh][i], a_diag[hh][i]) for i in range(2)] for hh in heads]

    prods = diag_products(qs.astype(BF16), k.astype(BF16))
    a_diag = [[jnp.where(lvl == GLA_LEVELS, prods[hh][i], 0.0) for i in range(2)] for hh in heads]
    p = g
    tot = g
    a_off = None
    for t in range(GLA_LEVELS):
        h = 1 << t
        if h < SUB:
            upper = (row & h) != 0
            e = jnp.where(upper, p, tot - p)
            a_diag = level_update(a_diag, t, jnp.where(upper, qs, k) * jnp.exp(e))
            t_lo = pltpu.roll(tot, h, 0)
            t_hi = pltpu.roll(tot, L - h, 0)
            p = p + jnp.where(upper, t_lo, 0.0)
            tot = tot + jnp.where(upper, t_lo, t_hi)
        else:
            p_lo, p_up = halves(p, h)
            t_lo, t_up = halves(tot, h)
            q_up = halves(qs, h)[1]
            k_lo = halves(k, h)[0]
            x_lo = [kk * jnp.exp(tl - pl_) for kk, tl, pl_ in zip(k_lo, t_lo, p_lo)]
            x_up = [qq * jnp.exp(pu) for qq, pu in zip(q_up, p_up)]
            if 2 * h < L:
                a_diag = level_update(a_diag, t, interleave(x_lo, x_up))
            else:
                xu, xl = x_up[0].astype(BF16), x_lo[0].astype(BF16)
                a_off = [_dot_nt(xu[:, hh * HK:(hh + 1) * HK], xl[:, hh * HK:(hh + 1) * HK]) for hh in heads]
            t_new = [tl + tu for tl, tu in zip(t_lo, t_up)]
            p = interleave(p_lo, [pu + tl for pu, tl in zip(p_up, t_lo)])
            tot = interleave(t_new, t_new)

    qe = (qs * jnp.exp(p)).astype(BF16)
    kd = (k * jnp.exp(tot - p)).astype(BF16)
    decay = jnp.exp(tot[0:1, :])
    outs = []
    for hh in heads:
        ks = slice(hh * HK, (hh + 1) * HK)
        vs = slice(hh * HV, (hh + 1) * HV)
        st = st_ref[st_base + hh]
        vh = v[:, vs]
        a_top = a_diag[hh][0].astype(BF16)
        a_bot = jnp.concatenate([a_off[hh], a_diag[hh][1]], axis=1).astype(BF16)
        o_intra = jnp.concatenate([_dot(a_top, vh[:H2]), _dot(a_bot, vh)], axis=0)
        o = _dot_nt(qe[:, ks], st.astype(BF16)) + o_intra
        st_ref[st_base + hh] = st * decay[:, ks] + _dot_tn(vh, kd[:, ks])
        rh = r[:, vs]
        outs.append((_rms(o, gon) * (rh * _sigmoid(rh))).astype(BF16))
    return jnp.concatenate(outs, axis=1)


def _gla_proj_kernel(x_ref, g_ref, w_ref, wg_ref, wup_ref, bgk_ref, gon_ref, lvl_ref, o_ref, st_ref):
    L = GLA_BLOCK
    wk = GLA_STEP_HEADS * GLA_HEAD_K
    wv = GLA_STEP_HEADS * GLA_HEAD_V

    @pl.when(pl.program_id(1) == 0)
    def _():
        st_ref[...] = jnp.zeros_like(st_ref)

    xn = _rms(x_ref[...], g_ref[...]).astype(BF16)
    q = _dot(xn, w_ref[:, 0:GLA_DK])
    k = _dot(xn, w_ref[:, GLA_DK:2 * GLA_DK])
    gl = _dot(xn, wg_ref[...])
    v = _dot(xn, w_ref[:, 2 * GLA_DK:2 * GLA_DK + GLA_DV]).astype(BF16)
    r = _dot(xn, w_ref[:, 2 * GLA_DK + GLA_DV:])
    gon = gon_ref[...]
    lvl = lvl_ref[...]
    for blk in range(x_ref.shape[0] // L):
        rows = slice(blk * L, (blk + 1) * L)
        for hp in range(GLA_HEADS // GLA_STEP_HEADS):
            ks = slice(hp * wk, (hp + 1) * wk)
            vs = slice(hp * wv, (hp + 1) * wv)
            o_ref[rows, vs] = _gla_block(q[rows, ks], k[rows, ks], v[rows, vs], r[rows, vs], gl[rows],
                                         wup_ref[:, ks], bgk_ref[:, ks], gon, lvl, st_ref, hp * GLA_STEP_HEADS)


def _gla_proj(x, g, w, w_gate, wup, bgk, gon, batch, seq, ts):
    t, d = x.shape
    nblk = seq // ts
    lvl = jnp.asarray(_gla_level_map(GLA_BLOCK // 2))
    const = lambda b, s: (0, 0)
    resident = lambda shape: pl.BlockSpec(shape, const, pipeline_mode=pl.Buffered(1))
    return pl.pallas_call(
        _gla_proj_kernel,
        name="gla_proj",
        grid=(batch, nblk),
        in_specs=[
            pl.BlockSpec((ts, d), lambda b, s: (b * nblk + s, 0)),
            resident((1, d)),
            resident(w.shape),
            resident((d, LANES)),
            resident((LANES, GLA_DK)),
            resident((1, GLA_DK)),
            resident((1, GLA_HEAD_V)),
            resident((GLA_BLOCK // 2, GLA_BLOCK // 2)),
        ],
        out_specs=pl.BlockSpec((ts, GLA_DV), lambda b, s: (b * nblk + s, 0)),
        out_shape=jax.ShapeDtypeStruct((t, GLA_DV), BF16),
        scratch_shapes=[pltpu.VMEM((GLA_HEADS, GLA_HEAD_V, GLA_HEAD_K), F32)],
        compiler_params=_cparams(("parallel", "arbitrary")),
    )(x, g.reshape(1, d), w, w_gate, wup, bgk.reshape(1, GLA_DK), gon.reshape(1, GLA_HEAD_V), lvl)


LOG2E = 1.4426950408889634
C_PIECES = 3


def _split3(x):
    hi = x.astype(BF16)
    r = x - hi.astype(F32)
    mid = r.astype(BF16)
    lo = (r - mid.astype(F32)).astype(BF16)
    return hi, mid, lo


def _fox_aug_matrices():
    n = FOX_HEADS * LANES
    pq = np.zeros((LANES, n), np.float32)
    pk = np.zeros((LANES, n), np.float32)
    oq = np.zeros((1, n), np.float32)
    ok = np.zeros((1, n), np.float32)
    for h in range(FOX_HEADS):
        o = h * LANES + (FOX_HEAD_DIM if h % 2 == 0 else 0)
        for t in range(C_PIECES):
            pq[FOX_HEADS * t + h, o + t] = 1.0
            ok[0, o + t] = 1.0
            pk[FOX_HEADS * t + h, o + C_PIECES + t] = -1.0
            oq[0, o + C_PIECES + t] = 1.0
    return pq, pk, oq, ok


def _fox_proj_prep_kernel(x_ref, g_ref, w_ref, wf_ref, bf_ref, gq_ref, gk_ref, bd_ref, tri_ref,
                          pq_ref, pk_ref, oq_ref, ok_ref, qt_ref, kop_ref, vt_ref, gate_ref, carry_ref):
    d = D_MODEL

    @pl.when(pl.program_id(1) == 0)
    def _():
        carry_ref[...] = jnp.zeros_like(carry_ref)

    xn = _rms(x_ref[...], g_ref[...]).astype(BF16)
    q = _dot(xn, w_ref[:, 0:d])
    k = _dot(xn, w_ref[:, d:2 * d])

    lf = _log_sigmoid(_dot(xn, wf_ref[...]) + bf_ref[...])
    hi, mid, lo = _split3(lf)
    tri = tri_ref[...]
    c_hm = _dot(tri, jnp.concatenate([hi, mid], axis=1))
    c = carry_ref[...] + (c_hm[:, :LANES] + c_hm[:, LANES:] + _dot(tri, lo))
    carry_ref[...] = c[c.shape[0] - 1:, :]

    lane = lax.broadcasted_iota(jnp.int32, (1, LANES), 1)
    hi, mid, lo = _split3(jnp.where(lane < FOX_HEADS, c * LOG2E, 0.0))
    cpack = (hi.astype(F32) + pltpu.roll(mid.astype(F32), FOX_HEADS, 1)
             + pltpu.roll(lo.astype(F32), 2 * FOX_HEADS, 1)).astype(BF16)
    aug_q = _dot(cpack, pq_ref[...]) + oq_ref[...]
    aug_k = _dot(cpack, pk_ref[...]) + ok_ref[...]

    v = _dot(xn, w_ref[:, 2 * d:3 * d])
    gate_ref[...] = _dot(xn, w_ref[:, 3 * d:4 * d]).astype(gate_ref.dtype)

    bd = bd_ref[...]

    def head_norm(x, gain):
        ss = _dot((x * x).astype(BF16), bd)
        return x * lax.rsqrt(ss * (1.0 / FOX_HEAD_DIM) + EPS) * gain

    halves = (lane < FOX_HEAD_DIM, lane >= FOX_HEAD_DIM)
    for pair in range(FOX_PAIRS):
        sl = slice(pair * LANES, (pair + 1) * LANES)
        if pair % 2 == 0:
            sl2 = slice(pair * LANES, (pair + 2) * LANES)
            qn2 = head_norm(q[:, sl2], gq_ref[:, sl2]) * (FOX_HEAD_DIM ** -0.5 * LOG2E)
            kn2 = head_norm(k[:, sl2], gk_ref[:, sl2])
        inner = slice((pair % 2) * LANES, (pair % 2 + 1) * LANES)
        qn, kn = qn2[:, inner], kn2[:, inner]
        for e in range(2):
            hs = slice((2 * pair + e) * LANES, (2 * pair + e + 1) * LANES)
            qt_ref[0, hs, :] = jnp.where(halves[e], qn, aug_q[:, hs]).T.astype(BF16)
            kop_ref[:, hs] = jnp.where(halves[e], kn, aug_k[:, hs]).astype(BF16)
        vt = v[:, sl].T.astype(BF16)
        ones = jnp.ones((FOX_ONES_ROWS, vt.shape[1]), BF16)
        for e in range(2):
            r0 = (2 * pair + e) * FOX_VT_ROWS
            vt_ref[0, r0:r0 + FOX_HEAD_DIM, :] = vt[e * FOX_HEAD_DIM:(e + 1) * FOX_HEAD_DIM]
            vt_ref[0, r0 + FOX_HEAD_DIM:r0 + FOX_VT_ROWS, :] = ones


def _fox_proj_prep(x, g, w, w_f, b_f, g_q, g_k, batch, seq, ts):
    t, d = x.shape
    nblk = seq // ts
    n_op = FOX_HEADS * LANES
    bd = jnp.asarray(np.kron(np.eye(2 * LANES // FOX_HEAD_DIM), np.ones((FOX_HEAD_DIM, FOX_HEAD_DIM))), BF16)
    tri = jnp.asarray(np.tril(np.ones((ts, ts))), BF16)
    pq, pk, oq, ok = _fox_aug_matrices()
    bf = jnp.zeros((1, LANES), F32).at[0, :FOX_HEADS].set(b_f)
    gq = jnp.tile(g_q, FOX_HEADS).reshape(1, d)
    gk = jnp.tile(g_k, FOX_HEADS).reshape(1, d)
    rows = lambda b, s: (b * nblk + s, 0)
    const = lambda b, s: (0, 0)
    resident = lambda shape: pl.BlockSpec(shape, const, pipeline_mode=pl.Buffered(1))
    return pl.pallas_call(
        _fox_proj_prep_kernel,
        name="fox_proj_prep",
        grid=(batch, nblk),
        in_specs=[
            pl.BlockSpec((ts, d), rows),
            resident((1, d)),
            resident((d, 4 * d)),
            resident((d, LANES)),
            resident((1, LANES)),
            resident((1, d)),
            resident((1, d)),
            resident((2 * LANES, 2 * LANES)),
            resident((ts, ts)),
            resident((LANES, n_op)),
            resident((LANES, n_op)),
            resident((1, n_op)),
            resident((1, n_op)),
        ],
        out_specs=[
            pl.BlockSpec((1, n_op, ts), lambda b, s: (b, 0, s)),
            pl.BlockSpec((ts, n_op), rows),
            pl.BlockSpec((1, FOX_HEADS * FOX_VT_ROWS, ts), lambda b, s: (b, 0, s)),
            pl.BlockSpec((ts, d), rows),
        ],
        out_shape=[
            jax.ShapeDtypeStruct((batch, n_op, seq), BF16),
            jax.ShapeDtypeStruct((t, n_op), BF16),
            jax.ShapeDtypeStruct((batch, FOX_HEADS * FOX_VT_ROWS, seq), BF16),
            jax.ShapeDtypeStruct((t, d), BF16),
        ],
        scratch_shapes=[pltpu.VMEM((1, LANES), F32)],
        compiler_params=_cparams(("parallel", "arbitrary")),
    )(x, g.reshape(1, d), w, w_f, bf, gq, gk, bd, tri,
      jnp.asarray(pq, BF16), jnp.asarray(pk, BF16), jnp.asarray(oq), jnp.asarray(ok))


def _fox_attn_kernel(q_ref, k_ref, vt_ref, gate_ref, o_ref, acc_ref, m_ref, s_ref, *, tq):
    half = FOX_HEAD_DIM
    nq = q_ref.shape[2] // tq
    hand = (2, 3)

    def q_operands(qi):
        cols = pl.ds(pl.multiple_of(qi * tq, tq), tq)
        return [q_ref[0, e * LANES:(e + 1) * LANES, cols] for e in range(2)]

    def put_scores(qq, kj, slot, e):
        start = pl.multiple_of(kj * tq, tq)
        s_ref[slot, e] = _dot(k_ref[pl.ds(start, tq), e * LANES:(e + 1) * LANES], qq[e])

    def block(kj, slot, e, diagonal):
        start = pl.multiple_of(kj * tq, tq)
        vop = vt_ref[0, e * FOX_VT_ROWS:(e + 1) * FOX_VT_ROWS, pl.ds(start, tq)]
        s = s_ref[slot, e]
        if diagonal:
            key = lax.broadcasted_iota(jnp.int32, (tq, tq), 0)
            qry = lax.broadcasted_iota(jnp.int32, (tq, tq), 1)
            s = jnp.where(key <= qry, s, -jnp.inf)
        m_prev = m_ref[e]
        m_new = jnp.maximum(m_prev, jnp.max(s, axis=0, keepdims=True))
        alpha = jnp.exp2(m_prev - m_new)
        p = jnp.exp2(s - m_new).astype(BF16)
        acc_ref[e] = acc_ref[e] * alpha + _dot(vop, p)
        m_ref[e] = m_new

    def step(nxt, cur, diagonal=False):
        for e in range(2):
            put_scores(nxt[0], nxt[1], nxt[2], e)
            block(cur[0], cur[1], e, diagonal)

    def q_block(qi, parity):
        q_rows = pl.ds(pl.multiple_of(qi * tq, tq), tq)
        qq = q_operands(qi)
        following = (q_operands(jnp.minimum(qi + 1, nq - 1)), 0, hand[1 - parity])

        acc_ref[...] = jnp.zeros_like(acc_ref)
        m_ref[...] = jnp.full(m_ref.shape, -1e30, F32)

        def middle(j0, count):
            for u in range(count):
                step((qq, j0 + u + 1, u % 2), (j0 + u, (u + 1) % 2))

        def general():
            step((qq, 1, 1), (0, hand[parity]))
            n_mid = qi - 1

            def quad_body(i, carry):
                middle(1 + 4 * i, 4)
                return carry

            lax.fori_loop(0, n_mid // 4, quad_body, 0)
            j0 = 1 + (n_mid // 4) * 4
            for rem in range(1 - parity, 4, 2):
                if rem:
                    pl.when(n_mid % 4 == rem)(functools.partial(middle, j0, rem))
            step(following, (qi, parity), True)

        if parity == 0:
            pl.when(qi == 0)(lambda: step(following, (0, hand[0]), True))
            pl.when(qi > 0)(general)
        else:
            general()

        outs = []
        for e in range(2):
            acc = acc_ref[e]
            denom = acc[half:half + 1]
            outs.append(acc[:half] * (1.0 / denom))
        o2 = jnp.concatenate(outs, axis=0).T
        o_ref[q_rows, :] = (o2 * _sigmoid(gate_ref[q_rows, :].astype(F32))).astype(o_ref.dtype)

    first = q_operands(0)
    for e in range(2):
        put_scores(first, 0, hand[0], e)

    def pair_body(jp, carry):
        q_block(2 * jp, 0)
        q_block(2 * jp + 1, 1)
        return carry

    lax.fori_loop(0, nq // 2, pair_body, 0)


def _fox_attn(qt, kop, vt, gate, batch, seq, tq):
    t = batch * seq
    return pl.pallas_call(
        functools.partial(_fox_attn_kernel, tq=tq),
        name="fox_attn",
        grid=(batch, FOX_PAIRS),
        in_specs=[
            pl.BlockSpec((1, 2 * LANES, seq), lambda b, hp: (b, hp, 0)),
            pl.BlockSpec((seq, 2 * LANES), lambda b, hp: (b, hp)),
            pl.BlockSpec((1, 2 * FOX_VT_ROWS, seq), lambda b, hp: (b, hp, 0)),
            pl.BlockSpec((seq, LANES), lambda b, hp: (b, hp)),
        ],
        out_specs=pl.BlockSpec((seq, LANES), lambda b, hp: (b, hp)),
        out_shape=jax.ShapeDtypeStruct((t, D_MODEL), BF16),
        scratch_shapes=[pltpu.VMEM((2, FOX_VT_ROWS, tq), F32), pltpu.VMEM((2, 1, tq), F32),
                        pltpu.VMEM((4, 2, tq, tq), F32)],
        compiler_params=_cparams(("parallel", "parallel")),
    )(qt, kop, vt, gate)


def kernel(x, l0_norm_mix, l0_w_in, l0_w_gk_up, l0_b_gk, l0_g_onorm, l0_w_out, l0_norm_ffn, l0_w_ff1, l0_w_ff2, l1_norm_mix, l1_w_in, l1_b_f, l1_g_q, l1_g_k, l1_w_out, l1_norm_ffn, l1_w_ff1, l1_w_ff2, final_norm):
    batch, seq, d = x.shape
    t = batch * seq
    xf = x.reshape(t, d)

    n_main = 2 * GLA_DK + 2 * GLA_DV
    w_main = l0_w_in[:, :n_main].astype(BF16)
    w_gate = jnp.zeros((d, LANES), F32).at[:, :GLA_GATE_RANK].set(l0_w_in[:, n_main:]).astype(BF16)
    w_up = jnp.zeros((LANES, GLA_DK), F32).at[:GLA_GATE_RANK].set(l0_w_gk_up)
    og = _gla_proj(xf, l0_norm_mix, w_main, w_gate, w_up, l0_b_gk, l0_g_onorm, batch, seq, 512)
    xf = _mix_out_mlp(og, l0_w_out.astype(BF16), xf, l0_norm_ffn, l0_w_ff1.astype(BF16),
                      l0_w_ff2.astype(BF16), final_norm, 512, 1024, False)

    w_qkv = l1_w_in[:, :3 * d]
    w_f = l1_w_in[:, 3 * d:3 * d + FOX_HEADS]
    w_g = l1_w_in[:, 3 * d + FOX_HEADS:]
    w_main = jnp.concatenate([w_qkv, w_g], axis=1).astype(BF16)
    w_fp = jnp.zeros((d, LANES), F32).at[:, :FOX_HEADS].set(w_f).astype(BF16)
    qt, kop, vt, gate = _fox_proj_prep(xf, l1_norm_mix, w_main, w_fp, l1_b_f, l1_g_q, l1_g_k, batch, seq, 512)
    oa = _fox_attn(qt, kop, vt, gate, batch, seq, 512)
    xf = _mix_out_mlp(oa, l1_w_out.astype(BF16), xf, l1_norm_ffn, l1_w_ff1.astype(BF16),
                      l1_w_ff2.astype(BF16), final_norm, 512, 1024, True)
    return xf.reshape(batch, seq, d)
```

```python
import functools

import numpy as np
import jax
import jax.numpy as jnp
from jax import lax
from jax.experimental import pallas as pl
from jax.experimental.pallas import tpu as pltpu

F32 = jnp.float32
BF16 = jnp.bfloat16

EPS = 1e-6
D_MODEL = 1024
D_FF = 4 * D_MODEL

GLA_HEADS = 4
GLA_DK = 512
GLA_DV = 1024
GLA_HEAD_K = 128
GLA_HEAD_V = 256
GLA_GATE_RANK = 16
GLA_GATE_NORM = 16.0
GLA_BLOCK = 256
GLA_LEVELS = 8
GLA_STEP_HEADS = 4

FOX_HEADS = 16
FOX_HEAD_DIM = 64
FOX_PAIRS = FOX_HEADS // 2
FOX_ONES_ROWS = 16
FOX_VT_ROWS = FOX_HEAD_DIM + FOX_ONES_ROWS

LANES = 128
VMEM_LIMIT = 56 * 1024 * 1024


def _cparams(sem):
    return pltpu.CompilerParams(dimension_semantics=sem, vmem_limit_bytes=VMEM_LIMIT)


def _dot(a, b):
    return jnp.dot(a, b, preferred_element_type=F32)


def _dot_nt(a, b):
    return lax.dot_general(a, b, (((1,), (1,)), ((), ())), preferred_element_type=F32)


def _dot_tn(a, b):
    return lax.dot_general(a, b, (((0,), (0,)), ((), ())), preferred_element_type=F32)


def _rms(x, g):
    ms = jnp.mean(x * x, axis=-1, keepdims=True)
    return x * lax.rsqrt(ms + EPS) * g


def _log_sigmoid(z):
    return jnp.minimum(z, 0.0) - jnp.log(1.0 + jnp.exp(-jnp.abs(z)))


def _sigmoid(z):
    return 0.5 * jnp.tanh(0.5 * z) + 0.5


def _mix_out_mlp_kernel(a_ref, wo_ref, x_ref, g_ref, w1_ref, w2_ref, gf_ref, o_ref, *, tf, final_norm):
    x1 = x_ref[...] + _dot(a_ref[...], wo_ref[...])
    xn = _rms(x1, g_ref[...]).astype(BF16)
    y = x1
    for c in range(w1_ref.shape[1] // tf):
        h = _dot(xn, w1_ref[:, c * tf:(c + 1) * tf])
        h = jnp.square(jnp.maximum(h, 0.0)).astype(BF16)
        y = y + _dot(h, w2_ref[c * tf:(c + 1) * tf, :])
    if final_norm:
        y = _rms(y, gf_ref[...])
    o_ref[...] = y


def _mix_out_mlp(a, w_out, x, g, w1, w2, gf, tm, tf, final_norm):
    t, d = x.shape
    ff = w1.shape[1]
    resident = lambda shape: pl.BlockSpec(shape, lambda i: (0, 0), pipeline_mode=pl.Buffered(1))
    return pl.pallas_call(
        functools.partial(_mix_out_mlp_kernel, tf=tf, final_norm=final_norm),
        name="mix_out_mlp",
        grid=(t // tm,),
        in_specs=[
            pl.BlockSpec((tm, d), lambda i: (i, 0)),
            resident((d, d)),
            pl.BlockSpec((tm, d), lambda i: (i, 0)),
            resident((1, d)),
            resident((d, ff)),
            resident((ff, d)),
            resident((1, d)),
        ],
        out_specs=pl.BlockSpec((tm, d), lambda i: (i, 0)),
        out_shape=jax.ShapeDtypeStruct((t, d), F32),
        compiler_params=_cparams(("parallel",)),
    )(a, w_out, x, g.reshape(1, d), w1, w2, gf.reshape(1, d))


def _gla_level_map(n):
    i = np.arange(n)[:, None]
    j = np.arange(n)[None, :]
    x = i ^ j
    lvl = np.zeros((n, n), np.int32)
    nz = x > 0
    lvl[nz] = np.floor(np.log2(x[nz])).astype(np.int32) + 1
    lvl = np.where(i > j, lvl, 0)
    lvl = np.where(i == j, GLA_LEVELS, lvl)
    return lvl.astype(np.int32)


def _gla_block(q, k, v, r, gl, wup, bgk, gon, lvl, st_ref, st_base):
    L = GLA_BLOCK
    HK, HV = GLA_HEAD_K, GLA_HEAD_V
    heads = range(GLA_STEP_HEADS)

    z = _dot(gl, wup) + bgk
    g = _log_sigmoid(z) * (1.0 / GLA_GATE_NORM)

    qs = q * (HK ** -0.5)
    H2 = L // 2
    SUB = 8
    row = lax.broadcasted_iota(jnp.int32, qs.shape, 0)

    def halves(x, h):
        lo = [x[r:r + h] for r in range(0, L, 2 * h)]
        up = [x[r + h:r + 2 * h] for r in range(0, L, 2 * h)]
        return lo, up

    def interleave(lo, up):
        return jnp.concatenate([piece for pair in zip(lo, up) for piece in pair], axis=0)

    def diag_products(xq, xk):
        return [[_dot_nt(xq[i * H2:(i + 1) * H2, hh * HK:(hh + 1) * HK],
                         xk[i * H2:(i + 1) * H2, hh * HK:(hh + 1) * HK]) for i in range(2)] for hh in heads]

    def level_update(a_diag, t, xs):
        xs = xs.astype(BF16)
        prods = diag_products(xs, xs)
        return [[jnp.where(lvl == t + 1, prods[hh][i], a_diag[hh][i]) for i in range(2)] for hh in heads]

    prods = diag_products(qs.astype(BF16), k.astype(BF16))
    a_diag = [[jnp.where(lvl == GLA_LEVELS, prods[hh][i], 0.0) for i in range(2)] for hh in heads]
    p = g
    tot = g
    a_off = None
    for t in range(GLA_LEVELS):
        h = 1 << t
        if h < SUB:
            upper = (row & h) != 0
            e = jnp.where(upper, p, tot - p)
            a_diag = level_update(a_diag, t, jnp.where(upper, qs, k) * jnp.exp(e))
            t_lo = pltpu.roll(tot, h, 0)
            t_hi = pltpu.roll(tot, L - h, 0)
            p = p + jnp.where(upper, t_lo, 0.0)
            tot = tot + jnp.where(upper, t_lo, t_hi)
        else:
            p_lo, p_up = halves(p, h)
            t_lo, t_up = halves(tot, h)
            q_up = halves(qs, h)[1]
            k_lo = halves(k, h)[0]
            x_lo = [kk * jnp.exp(tl - pl_) for kk, tl, pl_ in zip(k_lo, t_lo, p_lo)]
            x_up = [qq * jnp.exp(pu) for qq, pu in zip(q_up, p_up)]
            if 2 * h < L:
                a_diag = level_update(a_diag, t, interleave(x_lo, x_up))
            else:
                xu, xl = x_up[0].astype(BF16), x_lo[0].astype(BF16)
                a_off = [_dot_nt(xu[:, hh * HK:(hh + 1) * HK], xl[:, hh * HK:(hh + 1) * HK]) for hh in heads]
            t_new = [tl + tu for tl, tu in zip(t_lo, t_up)]
            p = interleave(p_lo, [pu + tl for pu, tl in zip(p_up, t_lo)])
            tot = interleave(t_new, t_new)

    qe = (qs * jnp.exp(p)).astype(BF16)
    kd = (k * jnp.exp(tot - p)).astype(BF16)
    decay = jnp.exp(tot[0:1, :])
    outs = []
    for hh in heads:
        ks = slice(hh * HK, (hh + 1) * HK)
        vs = slice(hh * HV, (hh + 1) * HV)
        st = st_ref[st_base + hh]
        vh = v[:, vs]
        a_top = a_diag[hh][0].astype(BF16)
        a_bot = jnp.concatenate([a_off[hh], a_diag[hh][1]], axis=1).astype(BF16)
        o_intra = jnp.concatenate([_dot(a_top, vh[:H2]), _dot(a_bot, vh)], axis=0)
        o = _dot_nt(qe[:, ks], st.astype(BF16)) + o_intra
        st_ref[st_base + hh] = st * decay[:, ks] + _dot_tn(vh, kd[:, ks])
        rh = r[:, vs]
        outs.append((_rms(o, gon) * (rh * _sigmoid(rh))).astype(BF16))
    return jnp.concatenate(outs, axis=1)


def _gla_proj_kernel(x_ref, g_ref, w_ref, wg_ref, wup_ref, bgk_ref, gon_ref, lvl_ref, o_ref, st_ref):
    L = GLA_BLOCK
    wk = GLA_STEP_HEADS * GLA_HEAD_K
    wv = GLA_STEP_HEADS * GLA_HEAD_V

    @pl.when(pl.program_id(1) == 0)
    def _():
        st_ref[...] = jnp.zeros_like(st_ref)

    xn = _rms(x_ref[...], g_ref[...]).astype(BF16)
    q = _dot(xn, w_ref[:, 0:GLA_DK])
    k = _dot(xn, w_ref[:, GLA_DK:2 * GLA_DK])
    gl = _dot(xn, wg_ref[...])
    v = _dot(xn, w_ref[:, 2 * GLA_DK:2 * GLA_DK + GLA_DV]).astype(BF16)
    r = _dot(xn, w_ref[:, 2 * GLA_DK + GLA_DV:])
    gon = gon_ref[...]
    lvl = lvl_ref[...]
    for blk in range(x_ref.shape[0] // L):
        rows = slice(blk * L, (blk + 1) * L)
        for hp in range(GLA_HEADS // GLA_STEP_HEADS):
            ks = slice(hp * wk, (hp + 1) * wk)
            vs = slice(hp * wv, (hp + 1) * wv)
            o_ref[rows, vs] = _gla_block(q[rows, ks], k[rows, ks], v[rows, vs], r[rows, vs], gl[rows],
                                         wup_ref[:, ks], bgk_ref[:, ks], gon, lvl, st_ref, hp * GLA_STEP_HEADS)


def _gla_proj(x, g, w, w_gate, wup, bgk, gon, batch, seq, ts):
    t, d = x.shape
    nblk = seq // ts
    lvl = jnp.asarray(_gla_level_map(GLA_BLOCK // 2))
    const = lambda b, s: (0, 0)
    resident = lambda shape: pl.BlockSpec(shape, const, pipeline_mode=pl.Buffered(1))
    return pl.pallas_call(
        _gla_proj_kernel,
        name="gla_proj",
        grid=(batch, nblk),
        in_specs=[
            pl.BlockSpec((ts, d), lambda b, s: (b * nblk + s, 0)),
            resident((1, d)),
            resident(w.shape),
            resident((d, LANES)),
            resident((LANES, GLA_DK)),
            resident((1, GLA_DK)),
            resident((1, GLA_HEAD_V)),
            resident((GLA_BLOCK // 2, GLA_BLOCK // 2)),
        ],
        out_specs=pl.BlockSpec((ts, GLA_DV), lambda b, s: (b * nblk + s, 0)),
        out_shape=jax.ShapeDtypeStruct((t, GLA_DV), BF16),
        scratch_shapes=[pltpu.VMEM((GLA_HEADS, GLA_HEAD_V, GLA_HEAD_K), F32)],
        compiler_params=_cparams(("parallel", "arbitrary")),
    )(x, g.reshape(1, d), w, w_gate, wup, bgk.reshape(1, GLA_DK), gon.reshape(1, GLA_HEAD_V), lvl)


LOG2E = 1.4426950408889634
C_PIECES = 3


def _split3(x):
    hi = x.astype(BF16)
    r = x - hi.astype(F32)
    mid = r.astype(BF16)
    lo = (r - mid.astype(F32)).astype(BF16)
    return hi, mid, lo


def _fox_aug_matrices():
    n = FOX_PAIRS * LANES
    pq = np.zeros((LANES, n), np.float32)
    pk = np.zeros((LANES, n), np.float32)
    oq = np.zeros((1, n), np.float32)
    ok = np.zeros((1, n), np.float32)
    for h in range(FOX_HEADS):
        o = (h // 2) * LANES + (FOX_HEAD_DIM if h % 2 == 0 else 0)
        for t in range(C_PIECES):
            pq[FOX_HEADS * t + h, o + t] = 1.0
            ok[0, o + t] = 1.0
            pk[FOX_HEADS * t + h, o + C_PIECES + t] = -1.0
            oq[0, o + C_PIECES + t] = 1.0
    return pq, pk, oq, ok


def _fox_proj_prep_kernel(x_ref, g_ref, w_ref, wo_ref, wf_ref, bf_ref, gq_ref, gk_ref, bd_ref, tri_ref,
                          pq_ref, pk_ref, oq_ref, ok_ref, qt_ref, kop_ref, vt_ref, gate_ref, carry_ref):
    d = D_MODEL

    @pl.when(pl.program_id(1) == 0)
    def _():
        carry_ref[...] = jnp.zeros_like(carry_ref)

    xn = _rms(x_ref[...], g_ref[...]).astype(BF16)
    q = _dot(xn, w_ref[:, 0:d])
    k = _dot(xn, w_ref[:, d:2 * d])

    lf = _log_sigmoid(_dot(xn, wf_ref[...]) + bf_ref[...])
    hi, mid, lo = _split3(lf)
    tri = tri_ref[...]
    c_hm = _dot(tri, jnp.concatenate([hi, mid], axis=1))
    c = carry_ref[...] + (c_hm[:, :LANES] + c_hm[:, LANES:] + _dot(tri, lo))
    carry_ref[...] = c[c.shape[0] - 1:, :]

    lane = lax.broadcasted_iota(jnp.int32, (1, LANES), 1)
    hi, mid, lo = _split3(jnp.where(lane < FOX_HEADS, c * LOG2E, 0.0))
    cpack = (hi.astype(F32) + pltpu.roll(mid.astype(F32), FOX_HEADS, 1)
             + pltpu.roll(lo.astype(F32), 2 * FOX_HEADS, 1)).astype(BF16)
    aug_q = _dot(cpack, pq_ref[...]) + oq_ref[...]
    aug_k = _dot(cpack, pk_ref[...]) + ok_ref[...]

    v = _dot(xn, w_ref[:, 2 * d:3 * d])
    gate_ref[...] = _dot(xn, wo_ref[...]).astype(gate_ref.dtype)

    bd = bd_ref[...]

    def head_norm(x, gain):
        ss = _dot((x * x).astype(BF16), bd)
        return x * lax.rsqrt(ss * (1.0 / FOX_HEAD_DIM) + EPS) * gain

    halves = (lane < FOX_HEAD_DIM, lane >= FOX_HEAD_DIM)
    for pair in range(FOX_PAIRS):
        sl = slice(pair * LANES, (pair + 1) * LANES)
        if pair % 2 == 0:
            sl2 = slice(pair * LANES, (pair + 2) * LANES)
            qn2 = head_norm(q[:, sl2], gq_ref[:, sl2]) * (FOX_HEAD_DIM ** -0.5 * LOG2E)
            kn2 = head_norm(k[:, sl2], gk_ref[:, sl2])
        inner = slice((pair % 2) * LANES, (pair % 2 + 1) * LANES)
        qn, kn = qn2[:, inner], kn2[:, inner]
        for e in range(2):
            hs = slice((2 * pair + e) * LANES, (2 * pair + e + 1) * LANES)
            qt_ref[0, hs, :] = jnp.where(halves[e], qn, aug_q[:, sl]).T.astype(BF16)
            kop_ref[:, hs] = jnp.where(halves[e], kn, aug_k[:, sl]).astype(BF16)
        vt = v[:, sl].T.astype(BF16)
        ones = jnp.ones((FOX_ONES_ROWS, vt.shape[1]), BF16)
        for e in range(2):
            r0 = (2 * pair + e) * FOX_VT_ROWS
            vt_ref[0, r0:r0 + FOX_HEAD_DIM, :] = vt[e * FOX_HEAD_DIM:(e + 1) * FOX_HEAD_DIM]
            vt_ref[0, r0 + FOX_HEAD_DIM:r0 + FOX_VT_ROWS, :] = ones


def _fox_proj_prep(x, g, w, w_o, w_f, b_f, g_q, g_k, batch, seq, ts):
    t, d = x.shape
    nblk = seq // ts
    n_op = FOX_HEADS * LANES
    bd = jnp.asarray(np.kron(np.eye(2 * LANES // FOX_HEAD_DIM), np.ones((FOX_HEAD_DIM, FOX_HEAD_DIM))), BF16)
    tri = jnp.asarray(np.tril(np.ones((ts, ts))), BF16)
    pq, pk, oq, ok = _fox_aug_matrices()
    bf = jnp.zeros((1, LANES), F32).at[0, :FOX_HEADS].set(b_f)
    gq = jnp.tile(g_q, FOX_HEADS).reshape(1, d)
    gk = jnp.tile(g_k, FOX_HEADS).reshape(1, d)
    rows = lambda b, s: (b * nblk + s, 0)
    const = lambda b, s: (0, 0)
    resident = lambda shape: pl.BlockSpec(shape, const, pipeline_mode=pl.Buffered(1))
    return pl.pallas_call(
        _fox_proj_prep_kernel,
        name="fox_proj_prep",
        grid=(batch, nblk),
        in_specs=[
            pl.BlockSpec((ts, d), rows),
            resident((1, d)),
            resident((d, 3 * d)),
            resident((d, d)),
            resident((d, LANES)),
            resident((1, LANES)),
            resident((1, d)),
            resident((1, d)),
            resident((2 * LANES, 2 * LANES)),
            resident((ts, ts)),
            resident((LANES, FOX_PAIRS * LANES)),
            resident((LANES, FOX_PAIRS * LANES)),
            resident((1, FOX_PAIRS * LANES)),
            resident((1, FOX_PAIRS * LANES)),
        ],
        out_specs=[
            pl.BlockSpec((1, n_op, ts), lambda b, s: (b, 0, s)),
            pl.BlockSpec((ts, n_op), rows),
            pl.BlockSpec((1, FOX_HEADS * FOX_VT_ROWS, ts), lambda b, s: (b, 0, s)),
            pl.BlockSpec((ts, d), rows),
        ],
        out_shape=[
            jax.ShapeDtypeStruct((batch, n_op, seq), BF16),
            jax.ShapeDtypeStruct((t, n_op), BF16),
            jax.ShapeDtypeStruct((batch, FOX_HEADS * FOX_VT_ROWS, seq), BF16),
            jax.ShapeDtypeStruct((t, d), BF16),
        ],
        scratch_shapes=[pltpu.VMEM((1, LANES), F32)],
        compiler_params=_cparams(("parallel", "arbitrary")),
    )(x, g.reshape(1, d), w, w_o, w_f, bf, gq, gk, bd, tri,
      jnp.asarray(pq, BF16), jnp.asarray(pk, BF16), jnp.asarray(oq), jnp.asarray(ok))


def _fox_attn_kernel(q_ref, k_ref, vt_ref, gate_ref, o_ref, acc_ref, m_ref, s_ref, *, tq):
    half = FOX_HEAD_DIM
    nq = q_ref.shape[2] // tq
    hand = (2, 3)

    def q_operands(qi):
        cols = pl.ds(pl.multiple_of(qi * tq, tq), tq)
        return [q_ref[0, e * LANES:(e + 1) * LANES, cols] for e in range(2)]

    def put_scores(qq, kj, slot, e):
        start = pl.multiple_of(kj * tq, tq)
        s_ref[slot, e] = _dot(k_ref[pl.ds(start, tq), e * LANES:(e + 1) * LANES], qq[e])

    def block(kj, slot, e, diagonal):
        start = pl.multiple_of(kj * tq, tq)
        vop = vt_ref[0, e * FOX_VT_ROWS:(e + 1) * FOX_VT_ROWS, pl.ds(start, tq)]
        s = s_ref[slot, e]
        if diagonal:
            key = lax.broadcasted_iota(jnp.int32, (tq, tq), 0)
            qry = lax.broadcasted_iota(jnp.int32, (tq, tq), 1)
            s = jnp.where(key <= qry, s, -jnp.inf)
        m_prev = m_ref[e]
        m_new = jnp.maximum(m_prev, jnp.max(s, axis=0, keepdims=True))
        alpha = jnp.exp2(m_prev - m_new)
        p = jnp.exp2(s - m_new).astype(BF16)
        acc_ref[e] = acc_ref[e] * alpha + _dot(vop, p)
        m_ref[e] = m_new

    def step(nxt, cur, diagonal=False):
        for e in range(2):
            put_scores(nxt[0], nxt[1], nxt[2], e)
            block(cur[0], cur[1], e, diagonal)

    def q_block(qi, parity):
        q_rows = pl.ds(pl.multiple_of(qi * tq, tq), tq)
        qq = q_operands(qi)
        following = (q_operands(jnp.minimum(qi + 1, nq - 1)), 0, hand[1 - parity])

        acc_ref[...] = jnp.zeros_like(acc_ref)
        m_ref[...] = jnp.full(m_ref.shape, -1e30, F32)

        def middle(j0, count):
            for u in range(count):
                step((qq, j0 + u + 1, u % 2), (j0 + u, (u + 1) % 2))

        def general():
            step((qq, 1, 1), (0, hand[parity]))
            n_mid = qi - 1

            def quad_body(i, carry):
                middle(1 + 4 * i, 4)
                return carry

            lax.fori_loop(0, n_mid // 4, quad_body, 0)
            j0 = 1 + (n_mid // 4) * 4
            for rem in range(1 - parity, 4, 2):
                if rem:
                    pl.when(n_mid % 4 == rem)(functools.partial(middle, j0, rem))
            step(following, (qi, parity), True)

        if parity == 0:
            pl.when(qi == 0)(lambda: step(following, (0, hand[0]), True))
            pl.when(qi > 0)(general)
        else:
            general()

        outs = []
        for e in range(2):
            acc = acc_ref[e]
            denom = acc[half:half + 1]
            outs.append(acc[:half] * (1.0 / denom))
        o2 = jnp.concatenate(outs, axis=0).T
        o_ref[q_rows, :] = (o2 * _sigmoid(gate_ref[q_rows, :].astype(F32))).astype(o_ref.dtype)

    first = q_operands(0)
    for e in range(2):
        put_scores(first, 0, hand[0], e)

    def pair_body(jp, carry):
        q_block(2 * jp, 0)
        q_block(2 * jp + 1, 1)
        return carry

    lax.fori_loop(0, nq // 2, pair_body, 0)


def _fox_attn(qt, kop, vt, gate, batch, seq, tq):
    t = batch * seq
    return pl.pallas_call(
        functools.partial(_fox_attn_kernel, tq=tq),
        name="fox_attn",
        grid=(batch, FOX_PAIRS),
        in_specs=[
            pl.BlockSpec((1, 2 * LANES, seq), lambda b, hp: (b, hp, 0)),
            pl.BlockSpec((seq, 2 * LANES), lambda b, hp: (b, hp)),
            pl.BlockSpec((1, 2 * FOX_VT_ROWS, seq), lambda b, hp: (b, hp, 0)),
            pl.BlockSpec((seq, LANES), lambda b, hp: (b, hp)),
        ],
        out_specs=pl.BlockSpec((seq, LANES), lambda b, hp: (b, hp)),
        out_shape=jax.ShapeDtypeStruct((t, D_MODEL), BF16),
        scratch_shapes=[pltpu.VMEM((2, FOX_VT_ROWS, tq), F32), pltpu.VMEM((2, 1, tq), F32),
                        pltpu.VMEM((4, 2, tq, tq), F32)],
        compiler_params=_cparams(("parallel", "parallel")),
    )(qt, kop, vt, gate)


def kernel(x, l0_norm_mix, l0_w_in, l0_w_gk_up, l0_b_gk, l0_g_onorm, l0_w_out, l0_norm_ffn, l0_w_ff1, l0_w_ff2, l1_norm_mix, l1_w_in, l1_b_f, l1_g_q, l1_g_k, l1_w_out, l1_norm_ffn, l1_w_ff1, l1_w_ff2, final_norm):
    batch, seq, d = x.shape
    t = batch * seq
    xf = x.reshape(t, d)

    n_main = 2 * GLA_DK + 2 * GLA_DV
    w_main = l0_w_in[:, :n_main].astype(BF16)
    w_gate = jnp.zeros((d, LANES), F32).at[:, :GLA_GATE_RANK].set(l0_w_in[:, n_main:]).astype(BF16)
    w_up = jnp.zeros((LANES, GLA_DK), F32).at[:GLA_GATE_RANK].set(l0_w_gk_up)
    og = _gla_proj(xf, l0_norm_mix, w_main, w_gate, w_up, l0_b_gk, l0_g_onorm, batch, seq, 512)
    xf = _mix_out_mlp(og, l0_w_out.astype(BF16), xf, l0_norm_ffn, l0_w_ff1.astype(BF16),
                      l0_w_ff2.astype(BF16), final_norm, 512, 1024, False)

    w_qkv = l1_w_in[:, :3 * d].astype(BF16)
    w_f = l1_w_in[:, 3 * d:3 * d + FOX_HEADS]
    w_g = l1_w_in[:, 3 * d + FOX_HEADS:].astype(BF16)
    w_fp = jnp.zeros((d, LANES), F32).at[:, :FOX_HEADS].set(w_f).astype(BF16)
    qt, kop, vt, gate = _fox_proj_prep(xf, l1_norm_mix, w_qkv, w_g, w_fp, l1_b_f, l1_g_q, l1_g_k, batch, seq, 512)
    oa = _fox_attn(qt, kop, vt, gate, batch, seq, 512)
    xf = _mix_out_mlp(oa, l1_w_out.astype(BF16), xf, l1_norm_ffn, l1_w_ff1.astype(BF16),
                      l1_w_ff2.astype(BF16), final_norm, 512, 1024, True)
    return xf.reshape(batch, seq, d)
```

```python
import functools

import numpy as np
import jax
import jax.numpy as jnp
from jax import lax
from jax.experimental import pallas as pl
from jax.experimental.pallas import tpu as pltpu

F32 = jnp.float32
BF16 = jnp.bfloat16

EPS = 1e-6
D_MODEL = 1024

GLA_HEADS = 4
GLA_DK = 512
GLA_DV = 1024
GLA_HEAD_K = 128
GLA_HEAD_V = 256
GLA_GATE_RANK = 16
GLA_GATE_NORM = 16.0
GLA_BLOCK = 256
GLA_LEVELS = 8
GLA_STEP_HEADS = 4

FOX_HEADS = 16
FOX_HEAD_DIM = 64
FOX_PAIRS = FOX_HEADS // 2
FOX_ONES_ROWS = 16
FOX_VT_ROWS = FOX_HEAD_DIM + FOX_ONES_ROWS

LANES = 128
VMEM_LIMIT = 56 * 1024 * 1024

ROW_TILE = 512
FF_CHUNK = 1024
ATTN_BLOCK = 512
NEG_BIG = -1e30


def _cparams(sem):
    return pltpu.CompilerParams(dimension_semantics=sem, vmem_limit_bytes=VMEM_LIMIT)


def _dot(a, b):
    return jnp.dot(a, b, preferred_element_type=F32)


def _dot_nt(a, b):
    return lax.dot_general(a, b, (((1,), (1,)), ((), ())), preferred_element_type=F32)


def _dot_tn(a, b):
    return lax.dot_general(a, b, (((0,), (0,)), ((), ())), preferred_element_type=F32)


def _rms(x, g):
    ms = jnp.mean(x * x, axis=-1, keepdims=True)
    return x * lax.rsqrt(ms + EPS) * g


def _log_sigmoid(z):
    return jnp.minimum(z, 0.0) - jnp.log(1.0 + jnp.exp(-jnp.abs(z)))


def _sigmoid(z):
    return 0.5 * jnp.tanh(0.5 * z) + 0.5


def _mix_out_mlp_kernel(a_ref, wo_ref, x_ref, g_ref, w1_ref, w2_ref, gf_ref, o_ref, *, tf, final_norm):
    x1 = x_ref[...] + _dot(a_ref[...], wo_ref[...])
    xn = _rms(x1, g_ref[...]).astype(BF16)
    y = x1
    for c in range(w1_ref.shape[1] // tf):
        h = _dot(xn, w1_ref[:, c * tf:(c + 1) * tf])
        h = jnp.square(jnp.maximum(h, 0.0)).astype(BF16)
        y = y + _dot(h, w2_ref[c * tf:(c + 1) * tf, :])
    if final_norm:
        y = _rms(y, gf_ref[...])
    o_ref[...] = y


def _mix_out_mlp(a, w_out, x, g, w1, w2, gf, tm, tf, final_norm):
    t, d = x.shape
    ff = w1.shape[1]
    resident = lambda shape: pl.BlockSpec(shape, lambda i: (0, 0), pipeline_mode=pl.Buffered(1))
    return pl.pallas_call(
        functools.partial(_mix_out_mlp_kernel, tf=tf, final_norm=final_norm),
        name="mix_out_mlp",
        grid=(t // tm,),
        in_specs=[
            pl.BlockSpec((tm, d), lambda i: (i, 0)),
            resident((d, d)),
            pl.BlockSpec((tm, d), lambda i: (i, 0)),
            resident((1, d)),
            resident((d, ff)),
            resident((ff, d)),
            resident((1, d)),
        ],
        out_specs=pl.BlockSpec((tm, d), lambda i: (i, 0)),
        out_shape=jax.ShapeDtypeStruct((t, d), F32),
        compiler_params=_cparams(("parallel",)),
    )(a, w_out, x, g.reshape(1, d), w1, w2, gf.reshape(1, d))


def _gla_level_map(n):
    i = np.arange(n)[:, None]
    j = np.arange(n)[None, :]
    x = i ^ j
    lvl = np.zeros((n, n), np.int32)
    nz = x > 0
    lvl[nz] = np.floor(np.log2(x[nz])).astype(np.int32) + 1
    lvl = np.where(i > j, lvl, 0)
    lvl = np.where(i == j, GLA_LEVELS, lvl)
    return lvl.astype(np.int32)


def _gla_block(q, k, v, r, gl, wup, bgk, gon, lvl, st_ref, st_base):
    L = GLA_BLOCK
    HK, HV = GLA_HEAD_K, GLA_HEAD_V
    heads = range(GLA_STEP_HEADS)

    z = _dot(gl, wup) + bgk
    g = _log_sigmoid(z) * (1.0 / GLA_GATE_NORM)

    qs = q * (HK ** -0.5)
    H2 = L // 2
    SUB = 8
    row = lax.broadcasted_iota(jnp.int32, qs.shape, 0)

    def halves(x, h):
        lo = [x[r:r + h] for r in range(0, L, 2 * h)]
        up = [x[r + h:r + 2 * h] for r in range(0, L, 2 * h)]
        return lo, up

    def interleave(lo, up):
        return jnp.concatenate([piece for pair in zip(lo, up) for piece in pair], axis=0)

    def diag_products(xq, xk):
        return [[_dot_nt(xq[i * H2:(i + 1) * H2, hh * HK:(hh + 1) * HK],
                         xk[i * H2:(i + 1) * H2, hh * HK:(hh + 1) * HK]) for i in range(2)] for hh in heads]

    def level_update(a_diag, t, xs):
        xs = xs.astype(BF16)
        prods = diag_products(xs, xs)
        return [[jnp.where(lvl == t + 1, prods[hh][i], a_diag[hh][i]) for i in range(2)] for hh in heads]

    prods = diag_products(qs.astype(BF16), k.astype(BF16))
    a_diag = [[jnp.where(lvl == GLA_LEVELS, prods[hh][i], 0.0) for i in range(2)] for hh in heads]
    p = g
    tot = g
    a_off = None
    for t in range(GLA_LEVELS):
        h = 1 << t
        if h < SUB:
            upper = (row & h) != 0
            e = jnp.where(upper, p, tot - p)
            a_diag = level_update(a_diag, t, jnp.where(upper, qs, k) * jnp.exp(e))
            t_lo = pltpu.roll(tot, h, 0)
            t_hi = pltpu.roll(tot, L - h, 0)
            p = p + jnp.where(upper, t_lo, 0.0)
            tot = tot + jnp.where(upper, t_lo, t_hi)
        else:
            p_lo, p_up = halves(p, h)
            t_lo, t_up = halves(tot, h)
            q_up = halves(qs, h)[1]
            k_lo = halves(k, h)[0]
            x_lo = [kk * jnp.exp(tl - pl_) for kk, tl, pl_ in zip(k_lo, t_lo, p_lo)]
            x_up = [qq * jnp.exp(pu) for qq, pu in zip(q_up, p_up)]
            if 2 * h < L:
                a_diag = level_update(a_diag, t, interleave(x_lo, x_up))
            else:
                xu, xl = x_up[0].astype(BF16), x_lo[0].astype(BF16)
                a_off = [_dot_nt(xu[:, hh * HK:(hh + 1) * HK], xl[:, hh * HK:(hh + 1) * HK]) for hh in heads]
            t_new = [tl + tu for tl, tu in zip(t_lo, t_up)]
            p = interleave(p_lo, [pu + tl for pu, tl in zip(p_up, t_lo)])
            tot = interleave(t_new, t_new)

    qe = (qs * jnp.exp(p)).astype(BF16)
    kd = (k * jnp.exp(tot - p)).astype(BF16)
    decay = jnp.exp(tot[0:1, :])
    outs = []
    for hh in heads:
        ks = slice(hh * HK, (hh + 1) * HK)
        vs = slice(hh * HV, (hh + 1) * HV)
        st = st_ref[st_base + hh]
        vh = v[:, vs]
        a_top = a_diag[hh][0].astype(BF16)
        a_bot = jnp.concatenate([a_off[hh], a_diag[hh][1]], axis=1).astype(BF16)
        o_intra = jnp.concatenate([_dot(a_top, vh[:H2]), _dot(a_bot, vh)], axis=0)
        o = _dot_nt(qe[:, ks], st.astype(BF16)) + o_intra
        st_ref[st_base + hh] = st * decay[:, ks] + _dot_tn(vh, kd[:, ks])
        rh = r[:, vs]
        outs.append((_rms(o, gon) * (rh * _sigmoid(rh))).astype(BF16))
    return jnp.concatenate(outs, axis=1)


def _gla_proj_kernel(x_ref, g_ref, w_ref, wg_ref, wup_ref, bgk_ref, gon_ref, lvl_ref, o_ref, st_ref):
    L = GLA_BLOCK
    wk = GLA_STEP_HEADS * GLA_HEAD_K
    wv = GLA_STEP_HEADS * GLA_HEAD_V

    @pl.when(pl.program_id(1) == 0)
    def _():
        st_ref[...] = jnp.zeros_like(st_ref)

    xn = _rms(x_ref[...], g_ref[...]).astype(BF16)
    q = _dot(xn, w_ref[:, 0:GLA_DK])
    k = _dot(xn, w_ref[:, GLA_DK:2 * GLA_DK])
    gl = _dot(xn, wg_ref[...])
    v = _dot(xn, w_ref[:, 2 * GLA_DK:2 * GLA_DK + GLA_DV]).astype(BF16)
    r = _dot(xn, w_ref[:, 2 * GLA_DK + GLA_DV:])
    gon = gon_ref[...]
    lvl = lvl_ref[...]
    for blk in range(x_ref.shape[0] // L):
        rows = slice(blk * L, (blk + 1) * L)
        for hp in range(GLA_HEADS // GLA_STEP_HEADS):
            ks = slice(hp * wk, (hp + 1) * wk)
            vs = slice(hp * wv, (hp + 1) * wv)
            o_ref[rows, vs] = _gla_block(q[rows, ks], k[rows, ks], v[rows, vs], r[rows, vs], gl[rows],
                                         wup_ref[:, ks], bgk_ref[:, ks], gon, lvl, st_ref, hp * GLA_STEP_HEADS)


def _gla_proj(x, g, w, w_gate, wup, bgk, gon, batch, seq, ts):
    t, d = x.shape
    nblk = seq // ts
    lvl = jnp.asarray(_gla_level_map(GLA_BLOCK // 2))
    const = lambda b, s: (0, 0)
    resident = lambda shape: pl.BlockSpec(shape, const, pipeline_mode=pl.Buffered(1))
    return pl.pallas_call(
        _gla_proj_kernel,
        name="gla_proj",
        grid=(batch, nblk),
        in_specs=[
            pl.BlockSpec((ts, d), lambda b, s: (b * nblk + s, 0)),
            resident((1, d)),
            resident(w.shape),
            resident((d, LANES)),
            resident((LANES, GLA_DK)),
            resident((1, GLA_DK)),
            resident((1, GLA_HEAD_V)),
            resident((GLA_BLOCK // 2, GLA_BLOCK // 2)),
        ],
        out_specs=pl.BlockSpec((ts, GLA_DV), lambda b, s: (b * nblk + s, 0)),
        out_shape=jax.ShapeDtypeStruct((t, GLA_DV), BF16),
        scratch_shapes=[pltpu.VMEM((GLA_HEADS, GLA_HEAD_V, GLA_HEAD_K), F32)],
        compiler_params=_cparams(("parallel", "arbitrary")),
    )(x, g.reshape(1, d), w, w_gate, wup, bgk.reshape(1, GLA_DK), gon.reshape(1, GLA_HEAD_V), lvl)


LOG2E = 1.4426950408889634
C_PIECES = 3


def _split3(x):
    hi = x.astype(BF16)
    r = x - hi.astype(F32)
    mid = r.astype(BF16)
    lo = (r - mid.astype(F32)).astype(BF16)
    return hi, mid, lo


def _fox_aug_matrices():
    n = FOX_PAIRS * LANES
    pq = np.zeros((LANES, n), np.float32)
    pk = np.zeros((LANES, n), np.float32)
    oq = np.zeros((1, n), np.float32)
    ok = np.zeros((1, n), np.float32)
    for h in range(FOX_HEADS):
        o = (h // 2) * LANES + (FOX_HEAD_DIM if h % 2 == 0 else 0)
        for t in range(C_PIECES):
            pq[FOX_HEADS * t + h, o + t] = 1.0
            ok[0, o + t] = 1.0
            pk[FOX_HEADS * t + h, o + C_PIECES + t] = -1.0
            oq[0, o + C_PIECES + t] = 1.0
    return pq, pk, oq, ok


def _fox_proj_prep_kernel(x_ref, g_ref, w_ref, wo_ref, wf_ref, bf_ref, gq_ref, gk_ref, bd_ref, tri_ref,
                          pq_ref, pk_ref, oq_ref, ok_ref, qt_ref, kop_ref, vt_ref, gate_ref, carry_ref):
    d = D_MODEL

    @pl.when(pl.program_id(1) == 0)
    def _():
        carry_ref[...] = jnp.zeros_like(carry_ref)

    xn = _rms(x_ref[...], g_ref[...]).astype(BF16)
    q = _dot(xn, w_ref[:, 0:d])
    k = _dot(xn, w_ref[:, d:2 * d])

    lf = _log_sigmoid(_dot(xn, wf_ref[...]) + bf_ref[...])
    hi, mid, lo = _split3(lf)
    tri = tri_ref[...]
    c_hm = _dot(tri, jnp.concatenate([hi, mid], axis=1))
    c = carry_ref[...] + (c_hm[:, :LANES] + c_hm[:, LANES:] + _dot(tri, lo))
    carry_ref[...] = c[c.shape[0] - 1:, :]

    lane = lax.broadcasted_iota(jnp.int32, (1, LANES), 1)
    hi, mid, lo = _split3(jnp.where(lane < FOX_HEADS, c * LOG2E, 0.0))
    cpack = (hi.astype(F32) + pltpu.roll(mid.astype(F32), FOX_HEADS, 1)
             + pltpu.roll(lo.astype(F32), 2 * FOX_HEADS, 1)).astype(BF16)
    aug_q = _dot(cpack, pq_ref[...]) + oq_ref[...]
    aug_k = _dot(cpack, pk_ref[...]) + ok_ref[...]

    v = _dot(xn, w_ref[:, 2 * d:3 * d])
    gate_ref[...] = _dot(xn, wo_ref[...]).astype(gate_ref.dtype)

    bd = bd_ref[...]

    def head_norm(x, gain):
        ss = _dot((x * x).astype(BF16), bd)
        return x * lax.rsqrt(ss * (1.0 / FOX_HEAD_DIM) + EPS) * gain

    halves = (lane < FOX_HEAD_DIM, lane >= FOX_HEAD_DIM)
    for pair in range(FOX_PAIRS):
        sl = slice(pair * LANES, (pair + 1) * LANES)
        if pair % 2 == 0:
            sl2 = slice(pair * LANES, (pair + 2) * LANES)
            qn2 = head_norm(q[:, sl2], gq_ref[:, sl2]) * (FOX_HEAD_DIM ** -0.5 * LOG2E)
            kn2 = head_norm(k[:, sl2], gk_ref[:, sl2])
        inner = slice((pair % 2) * LANES, (pair % 2 + 1) * LANES)
        qn, kn = qn2[:, inner], kn2[:, inner]
        for e in range(2):
            hs = slice((2 * pair + e) * LANES, (2 * pair + e + 1) * LANES)
            qt_ref[0, hs, :] = jnp.where(halves[e], qn, aug_q[:, sl]).T.astype(BF16)
            kop_ref[:, hs] = jnp.where(halves[e], kn, aug_k[:, sl]).astype(BF16)
        vt = v[:, sl].T.astype(BF16)
        ones = jnp.ones((FOX_ONES_ROWS, vt.shape[1]), BF16)
        for e in range(2):
            r0 = (2 * pair + e) * FOX_VT_ROWS
            vt_ref[0, r0:r0 + FOX_HEAD_DIM, :] = vt[e * FOX_HEAD_DIM:(e + 1) * FOX_HEAD_DIM]
            vt_ref[0, r0 + FOX_HEAD_DIM:r0 + FOX_VT_ROWS, :] = ones


def _fox_proj_prep(x, g, w, w_o, w_f, b_f, g_q, g_k, batch, seq, ts):
    t, d = x.shape
    nblk = seq // ts
    n_op = FOX_HEADS * LANES
    bd = jnp.asarray(np.kron(np.eye(2 * LANES // FOX_HEAD_DIM), np.ones((FOX_HEAD_DIM, FOX_HEAD_DIM))), BF16)
    tri = jnp.asarray(np.tril(np.ones((ts, ts))), BF16)
    pq, pk, oq, ok = _fox_aug_matrices()
    bf = jnp.zeros((1, LANES), F32).at[0, :FOX_HEADS].set(b_f)
    gq = jnp.tile(g_q, FOX_HEADS).reshape(1, d)
    gk = jnp.tile(g_k, FOX_HEADS).reshape(1, d)
    rows = lambda b, s: (b * nblk + s, 0)
    const = lambda b, s: (0, 0)
    resident = lambda shape: pl.BlockSpec(shape, const, pipeline_mode=pl.Buffered(1))
    return pl.pallas_call(
        _fox_proj_prep_kernel,
        name="fox_proj_prep",
        grid=(batch, nblk),
        in_specs=[
            pl.BlockSpec((ts, d), rows),
            resident((1, d)),
            resident((d, 3 * d)),
            resident((d, d)),
            resident((d, LANES)),
            resident((1, LANES)),
            resident((1, d)),
            resident((1, d)),
            resident((2 * LANES, 2 * LANES)),
            resident((ts, ts)),
            resident((LANES, FOX_PAIRS * LANES)),
            resident((LANES, FOX_PAIRS * LANES)),
            resident((1, FOX_PAIRS * LANES)),
            resident((1, FOX_PAIRS * LANES)),
        ],
        out_specs=[
            pl.BlockSpec((1, n_op, ts), lambda b, s: (b, 0, s)),
            pl.BlockSpec((ts, n_op), rows),
            pl.BlockSpec((1, FOX_HEADS * FOX_VT_ROWS, ts), lambda b, s: (b, 0, s)),
            pl.BlockSpec((ts, d), rows),
        ],
        out_shape=[
            jax.ShapeDtypeStruct((batch, n_op, seq), BF16),
            jax.ShapeDtypeStruct((t, n_op), BF16),
            jax.ShapeDtypeStruct((batch, FOX_HEADS * FOX_VT_ROWS, seq), BF16),
            jax.ShapeDtypeStruct((t, d), BF16),
        ],
        scratch_shapes=[pltpu.VMEM((1, LANES), F32)],
        compiler_params=_cparams(("parallel", "arbitrary")),
    )(x, g.reshape(1, d), w, w_o, w_f, bf, gq, gk, bd, tri,
      jnp.asarray(pq, BF16), jnp.asarray(pk, BF16), jnp.asarray(oq), jnp.asarray(ok))


def _fox_attn_kernel(q_ref, k_ref, vt_ref, gate_ref, o_ref, acc_ref, m_ref, s_ref, *, tq):
    half = FOX_HEAD_DIM
    nq = q_ref.shape[2] // tq
    hand = (2, 3)

    def q_operands(qi):
        cols = pl.ds(pl.multiple_of(qi * tq, tq), tq)
        return [q_ref[0, e * LANES:(e + 1) * LANES, cols] for e in range(2)]

    def put_scores(qq, kj, slot, e):
        start = pl.multiple_of(kj * tq, tq)
        s_ref[slot, e] = _dot(k_ref[pl.ds(start, tq), e * LANES:(e + 1) * LANES], qq[e])

    def block(kj, slot, e, diagonal):
        start = pl.multiple_of(kj * tq, tq)
        vop = vt_ref[0, e * FOX_VT_ROWS:(e + 1) * FOX_VT_ROWS, pl.ds(start, tq)]
        s = s_ref[slot, e]
        if diagonal:
            key = lax.broadcasted_iota(jnp.int32, (tq, tq), 0)
            qry = lax.broadcasted_iota(jnp.int32, (tq, tq), 1)
            s = jnp.where(key <= qry, s, -jnp.inf)
        m_prev = m_ref[e]
        m_new = jnp.maximum(m_prev, jnp.max(s, axis=0, keepdims=True))
        alpha = jnp.exp2(m_prev - m_new)
        p = jnp.exp2(s - m_new).astype(BF16)
        acc_ref[e] = acc_ref[e] * alpha + _dot(vop, p)
        m_ref[e] = m_new

    def step(nxt, cur, diagonal=False):
        for e in range(2):
            put_scores(nxt[0], nxt[1], nxt[2], e)
            block(cur[0], cur[1], e, diagonal)

    def q_block(qi, parity):
        q_rows = pl.ds(pl.multiple_of(qi * tq, tq), tq)
        qq = q_operands(qi)
        following = (q_operands(jnp.minimum(qi + 1, nq - 1)), 0, hand[1 - parity])

        acc_ref[...] = jnp.zeros_like(acc_ref)
        m_ref[...] = jnp.full(m_ref.shape, NEG_BIG, F32)

        def middle(j0, count):
            for u in range(count):
                step((qq, j0 + u + 1, u % 2), (j0 + u, (u + 1) % 2))

        def general():
            step((qq, 1, 1), (0, hand[parity]))
            n_mid = qi - 1

            def quad_body(i, carry):
                middle(1 + 4 * i, 4)
                return carry

            lax.fori_loop(0, n_mid // 4, quad_body, 0)
            j0 = 1 + (n_mid // 4) * 4
            for rem in range(1 - parity, 4, 2):
                if rem:
                    pl.when(n_mid % 4 == rem)(functools.partial(middle, j0, rem))
            step(following, (qi, parity), True)

        if parity == 0:
            pl.when(qi == 0)(lambda: step(following, (0, hand[0]), True))
            pl.when(qi > 0)(general)
        else:
            general()

        outs = []
        for e in range(2):
            acc = acc_ref[e]
            denom = acc[half:half + 1]
            outs.append(acc[:half] * (1.0 / denom))
        o2 = jnp.concatenate(outs, axis=0).T
        o_ref[q_rows, :] = (o2 * _sigmoid(gate_ref[q_rows, :].astype(F32))).astype(o_ref.dtype)

    first = q_operands(0)
    for e in range(2):
        put_scores(first, 0, hand[0], e)

    def pair_body(jp, carry):
        q_block(2 * jp, 0)
        q_block(2 * jp + 1, 1)
        return carry

    lax.fori_loop(0, nq // 2, pair_body, 0)


def _fox_attn(qt, kop, vt, gate, batch, seq, tq):
    t = batch * seq
    return pl.pallas_call(
        functools.partial(_fox_attn_kernel, tq=tq),
        name="fox_attn",
        grid=(batch, FOX_PAIRS),
        in_specs=[
            pl.BlockSpec((1, 2 * LANES, seq), lambda b, hp: (b, hp, 0)),
            pl.BlockSpec((seq, 2 * LANES), lambda b, hp: (b, hp)),
            pl.BlockSpec((1, 2 * FOX_VT_ROWS, seq), lambda b, hp: (b, hp, 0)),
            pl.BlockSpec((seq, LANES), lambda b, hp: (b, hp)),
        ],
        out_specs=pl.BlockSpec((seq, LANES), lambda b, hp: (b, hp)),
        out_shape=jax.ShapeDtypeStruct((t, D_MODEL), BF16),
        scratch_shapes=[pltpu.VMEM((2, FOX_VT_ROWS, tq), F32), pltpu.VMEM((2, 1, tq), F32),
                        pltpu.VMEM((4, 2, tq, tq), F32)],
        compiler_params=_cparams(("parallel", "parallel")),
    )(qt, kop, vt, gate)


def kernel(x, l0_norm_mix, l0_w_in, l0_w_gk_up, l0_b_gk, l0_g_onorm, l0_w_out, l0_norm_ffn, l0_w_ff1, l0_w_ff2, l1_norm_mix, l1_w_in, l1_b_f, l1_g_q, l1_g_k, l1_w_out, l1_norm_ffn, l1_w_ff1, l1_w_ff2, final_norm):
    batch, seq, d = x.shape
    t = batch * seq
    xf = x.reshape(t, d)

    n_main = 2 * GLA_DK + 2 * GLA_DV
    w_main = l0_w_in[:, :n_main].astype(BF16)
    w_gate = jnp.zeros((d, LANES), F32).at[:, :GLA_GATE_RANK].set(l0_w_in[:, n_main:]).astype(BF16)
    w_up = jnp.zeros((LANES, GLA_DK), F32).at[:GLA_GATE_RANK].set(l0_w_gk_up)
    og = _gla_proj(xf, l0_norm_mix, w_main, w_gate, w_up, l0_b_gk, l0_g_onorm, batch, seq, ROW_TILE)
    xf = _mix_out_mlp(og, l0_w_out.astype(BF16), xf, l0_norm_ffn, l0_w_ff1.astype(BF16),
                      l0_w_ff2.astype(BF16), final_norm, ROW_TILE, FF_CHUNK, False)

    w_qkv = l1_w_in[:, :3 * d].astype(BF16)
    w_f = l1_w_in[:, 3 * d:3 * d + FOX_HEADS]
    w_g = l1_w_in[:, 3 * d + FOX_HEADS:].astype(BF16)
    w_fp = jnp.zeros((d, LANES), F32).at[:, :FOX_HEADS].set(w_f).astype(BF16)
    qt, kop, vt, gate = _fox_proj_prep(xf, l1_norm_mix, w_qkv, w_g, w_fp, l1_b_f, l1_g_q, l1_g_k, batch, seq, ROW_TILE)
    oa = _fox_attn(qt, kop, vt, gate, batch, seq, ATTN_BLOCK)
    xf = _mix_out_mlp(oa, l1_w_out.astype(BF16), xf, l1_norm_ffn, l1_w_ff1.astype(BF16),
                      l1_w_ff2.astype(BF16), final_norm, ROW_TILE, FF_CHUNK, True)
    return xf.reshape(batch, seq, d)
```

```python
import functools

import numpy as np
import jax
import jax.numpy as jnp
from jax import lax
from jax.experimental import pallas as pl
from jax.experimental.pallas import tpu as pltpu

F32 = jnp.float32
BF16 = jnp.bfloat16

EPS = 1e-6
D_MODEL = 1024

GLA_HEADS = 4
GLA_DK = 512
GLA_DV = 1024
GLA_HEAD_K = 128
GLA_HEAD_V = 256
GLA_GATE_RANK = 16
GLA_GATE_NORM = 16.0
GLA_BLOCK = 256
GLA_LEVELS = 8
GLA_STEP_HEADS = 4

FOX_HEADS = 16
FOX_HEAD_DIM = 64
FOX_PAIRS = FOX_HEADS // 2
FOX_ONES_ROWS = 16
FOX_VT_ROWS = FOX_HEAD_DIM + FOX_ONES_ROWS

LANES = 128
VMEM_LIMIT = 56 * 1024 * 1024

ROW_TILE = 512
FF_CHUNK = 1024
ATTN_BLOCK = 512
ATTN_UNROLL = 8
NEG_BIG = -1e30


def _cparams(sem):
    return pltpu.CompilerParams(dimension_semantics=sem, vmem_limit_bytes=VMEM_LIMIT)


def _dot(a, b):
    return jnp.dot(a, b, preferred_element_type=F32)


def _dot_nt(a, b):
    return lax.dot_general(a, b, (((1,), (1,)), ((), ())), preferred_element_type=F32)


def _dot_tn(a, b):
    return lax.dot_general(a, b, (((0,), (0,)), ((), ())), preferred_element_type=F32)


def _rms(x, g):
    ms = jnp.mean(x * x, axis=-1, keepdims=True)
    return x * lax.rsqrt(ms + EPS) * g


def _log_sigmoid(z):
    return jnp.minimum(z, 0.0) - jnp.log(1.0 + jnp.exp(-jnp.abs(z)))


def _sigmoid(z):
    return 0.5 * jnp.tanh(0.5 * z) + 0.5


def _mix_out_mlp_kernel(a_ref, wo_ref, x_ref, g_ref, w1_ref, w2_ref, gf_ref, o_ref, *, tf, final_norm):
    x1 = x_ref[...] + _dot(a_ref[...], wo_ref[...])
    xn = _rms(x1, g_ref[...]).astype(BF16)
    y = x1
    for c in range(w1_ref.shape[1] // tf):
        h = _dot(xn, w1_ref[:, c * tf:(c + 1) * tf])
        h = jnp.square(jnp.maximum(h, 0.0)).astype(BF16)
        y = y + _dot(h, w2_ref[c * tf:(c + 1) * tf, :])
    if final_norm:
        y = _rms(y, gf_ref[...])
    o_ref[...] = y


def _mix_out_mlp(a, w_out, x, g, w1, w2, gf, tm, tf, final_norm):
    t, d = x.shape
    ff = w1.shape[1]
    resident = lambda shape: pl.BlockSpec(shape, lambda i: (0, 0), pipeline_mode=pl.Buffered(1))
    return pl.pallas_call(
        functools.partial(_mix_out_mlp_kernel, tf=tf, final_norm=final_norm),
        name="mix_out_mlp",
        grid=(t // tm,),
        in_specs=[
            pl.BlockSpec((tm, d), lambda i: (i, 0)),
            resident((d, d)),
            pl.BlockSpec((tm, d), lambda i: (i, 0)),
            resident((1, d)),
            resident((d, ff)),
            resident((ff, d)),
            resident((1, d)),
        ],
        out_specs=pl.BlockSpec((tm, d), lambda i: (i, 0)),
        out_shape=jax.ShapeDtypeStruct((t, d), F32),
        compiler_params=_cparams(("parallel",)),
    )(a, w_out, x, g.reshape(1, d), w1, w2, gf.reshape(1, d))


def _gla_level_map(n):
    i = np.arange(n)[:, None]
    j = np.arange(n)[None, :]
    x = i ^ j
    lvl = np.zeros((n, n), np.int32)
    nz = x > 0
    lvl[nz] = np.floor(np.log2(x[nz])).astype(np.int32) + 1
    lvl = np.where(i > j, lvl, 0)
    lvl = np.where(i == j, GLA_LEVELS, lvl)
    return lvl.astype(np.int32)


def _gla_block(q, k, v, r, gl, wup, bgk, gon, lvl, st_ref, st_base):
    L = GLA_BLOCK
    HK, HV = GLA_HEAD_K, GLA_HEAD_V
    heads = range(GLA_STEP_HEADS)

    z = _dot(gl, wup) + bgk
    g = _log_sigmoid(z) * (1.0 / GLA_GATE_NORM)

    qs = q * (HK ** -0.5)
    H2 = L // 2
    SUB = 8
    row = lax.broadcasted_iota(jnp.int32, qs.shape, 0)

    def halves(x, h):
        lo = [x[r:r + h] for r in range(0, L, 2 * h)]
        up = [x[r + h:r + 2 * h] for r in range(0, L, 2 * h)]
        return lo, up

    def interleave(lo, up):
        return jnp.concatenate([piece for pair in zip(lo, up) for piece in pair], axis=0)

    def diag_products(xq, xk):
        return [[_dot_nt(xq[i * H2:(i + 1) * H2, hh * HK:(hh + 1) * HK],
                         xk[i * H2:(i + 1) * H2, hh * HK:(hh + 1) * HK]) for i in range(2)] for hh in heads]

    def level_update(a_diag, t, xs):
        xs = xs.astype(BF16)
        prods = diag_products(xs, xs)
        return [[jnp.where(lvl == t + 1, prods[hh][i], a_diag[hh][i]) for i in range(2)] for hh in heads]

    prods = diag_products(qs.astype(BF16), k.astype(BF16))
    a_diag = [[jnp.where(lvl == GLA_LEVELS, prods[hh][i], 0.0) for i in range(2)] for hh in heads]
    p = g
    tot = g
    a_off = None
    for t in range(GLA_LEVELS):
        h = 1 << t
        if h < SUB:
            upper = (row & h) != 0
            e = jnp.where(upper, p, tot - p)
            a_diag = level_update(a_diag, t, jnp.where(upper, qs, k) * jnp.exp(e))
            t_lo = pltpu.roll(tot, h, 0)
            t_hi = pltpu.roll(tot, L - h, 0)
            p = p + jnp.where(upper, t_lo, 0.0)
            tot = tot + jnp.where(upper, t_lo, t_hi)
        else:
            p_lo, p_up = halves(p, h)
            t_lo, t_up = halves(tot, h)
            q_up = halves(qs, h)[1]
            k_lo = halves(k, h)[0]
            x_lo = [kk * jnp.exp(tl - pl_) for kk, tl, pl_ in zip(k_lo, t_lo, p_lo)]
            x_up = [qq * jnp.exp(pu) for qq, pu in zip(q_up, p_up)]
            if 2 * h < L:
                a_diag = level_update(a_diag, t, interleave(x_lo, x_up))
            else:
                xu, xl = x_up[0].astype(BF16), x_lo[0].astype(BF16)
                a_off = [_dot_nt(xu[:, hh * HK:(hh + 1) * HK], xl[:, hh * HK:(hh + 1) * HK]) for hh in heads]
            t_new = [tl + tu for tl, tu in zip(t_lo, t_up)]
            p = interleave(p_lo, [pu + tl for pu, tl in zip(p_up, t_lo)])
            tot = interleave(t_new, t_new)

    qe = (qs * jnp.exp(p)).astype(BF16)
    kd = (k * jnp.exp(tot - p)).astype(BF16)
    decay = jnp.exp(tot[0:1, :])
    outs = []
    for hh in heads:
        ks = slice(hh * HK, (hh + 1) * HK)
        vs = slice(hh * HV, (hh + 1) * HV)
        st = st_ref[st_base + hh]
        vh = v[:, vs]
        a_top = a_diag[hh][0].astype(BF16)
        a_bot = jnp.concatenate([a_off[hh], a_diag[hh][1]], axis=1).astype(BF16)
        o_intra = jnp.concatenate([_dot(a_top, vh[:H2]), _dot(a_bot, vh)], axis=0)
        o = _dot_nt(qe[:, ks], st.astype(BF16)) + o_intra
        st_ref[st_base + hh] = st * decay[:, ks] + _dot_tn(vh, kd[:, ks])
        rh = r[:, vs]
        outs.append((_rms(o, gon) * (rh * _sigmoid(rh))).astype(BF16))
    return jnp.concatenate(outs, axis=1)


def _gla_proj_kernel(x_ref, g_ref, w_ref, wg_ref, wup_ref, bgk_ref, gon_ref, lvl_ref, o_ref, st_ref):
    L = GLA_BLOCK
    wk = GLA_STEP_HEADS * GLA_HEAD_K
    wv = GLA_STEP_HEADS * GLA_HEAD_V

    @pl.when(pl.program_id(1) == 0)
    def _():
        st_ref[...] = jnp.zeros_like(st_ref)

    xn = _rms(x_ref[...], g_ref[...]).astype(BF16)
    q = _dot(xn, w_ref[:, 0:GLA_DK])
    k = _dot(xn, w_ref[:, GLA_DK:2 * GLA_DK])
    gl = _dot(xn, wg_ref[...])
    v = _dot(xn, w_ref[:, 2 * GLA_DK:2 * GLA_DK + GLA_DV]).astype(BF16)
    r = _dot(xn, w_ref[:, 2 * GLA_DK + GLA_DV:])
    gon = gon_ref[...]
    lvl = lvl_ref[...]
    for blk in range(x_ref.shape[0] // L):
        rows = slice(blk * L, (blk + 1) * L)
        for hp in range(GLA_HEADS // GLA_STEP_HEADS):
            ks = slice(hp * wk, (hp + 1) * wk)
            vs = slice(hp * wv, (hp + 1) * wv)
            o_ref[rows, vs] = _gla_block(q[rows, ks], k[rows, ks], v[rows, vs], r[rows, vs], gl[rows],
                                         wup_ref[:, ks], bgk_ref[:, ks], gon, lvl, st_ref, hp * GLA_STEP_HEADS)


def _gla_proj(x, g, w, w_gate, wup, bgk, gon, batch, seq, ts):
    t, d = x.shape
    nblk = seq // ts
    lvl = jnp.asarray(_gla_level_map(GLA_BLOCK // 2))
    const = lambda b, s: (0, 0)
    resident = lambda shape: pl.BlockSpec(shape, const, pipeline_mode=pl.Buffered(1))
    return pl.pallas_call(
        _gla_proj_kernel,
        name="gla_proj",
        grid=(batch, nblk),
        in_specs=[
            pl.BlockSpec((ts, d), lambda b, s: (b * nblk + s, 0)),
            resident((1, d)),
            resident(w.shape),
            resident((d, LANES)),
            resident((LANES, GLA_DK)),
            resident((1, GLA_DK)),
            resident((1, GLA_HEAD_V)),
            resident((GLA_BLOCK // 2, GLA_BLOCK // 2)),
        ],
        out_specs=pl.BlockSpec((ts, GLA_DV), lambda b, s: (b * nblk + s, 0)),
        out_shape=jax.ShapeDtypeStruct((t, GLA_DV), BF16),
        scratch_shapes=[pltpu.VMEM((GLA_HEADS, GLA_HEAD_V, GLA_HEAD_K), F32)],
        compiler_params=_cparams(("parallel", "arbitrary")),
    )(x, g.reshape(1, d), w, w_gate, wup, bgk.reshape(1, GLA_DK), gon.reshape(1, GLA_HEAD_V), lvl)


LOG2E = 1.4426950408889634
C_PIECES = 3


def _split3(x):
    hi = x.astype(BF16)
    r = x - hi.astype(F32)
    mid = r.astype(BF16)
    lo = (r - mid.astype(F32)).astype(BF16)
    return hi, mid, lo


def _fox_aug_matrices():
    n = FOX_PAIRS * LANES
    pq = np.zeros((LANES, n), np.float32)
    pk = np.zeros((LANES, n), np.float32)
    oq = np.zeros((1, n), np.float32)
    ok = np.zeros((1, n), np.float32)
    for h in range(FOX_HEADS):
        o = (h // 2) * LANES + (FOX_HEAD_DIM if h % 2 == 0 else 0)
        for t in range(C_PIECES):
            pq[FOX_HEADS * t + h, o + t] = 1.0
            ok[0, o + t] = 1.0
            pk[FOX_HEADS * t + h, o + C_PIECES + t] = -1.0
            oq[0, o + C_PIECES + t] = 1.0
    return pq, pk, oq, ok


def _fox_proj_prep_kernel(x_ref, g_ref, w_ref, wo_ref, wf_ref, bf_ref, gq_ref, gk_ref, bd_ref, tri_ref,
                          pq_ref, pk_ref, oq_ref, ok_ref, qt_ref, kop_ref, vt_ref, gate_ref, carry_ref):
    d = D_MODEL

    @pl.when(pl.program_id(1) == 0)
    def _():
        carry_ref[...] = jnp.zeros_like(carry_ref)

    xn = _rms(x_ref[...], g_ref[...]).astype(BF16)
    q = _dot(xn, w_ref[:, 0:d])
    k = _dot(xn, w_ref[:, d:2 * d])

    lf = _log_sigmoid(_dot(xn, wf_ref[...]) + bf_ref[...])
    hi, mid, lo = _split3(lf)
    tri = tri_ref[...]
    c_hm = _dot(tri, jnp.concatenate([hi, mid], axis=1))
    c = carry_ref[...] + (c_hm[:, :LANES] + c_hm[:, LANES:] + _dot(tri, lo))
    carry_ref[...] = c[c.shape[0] - 1:, :]

    lane = lax.broadcasted_iota(jnp.int32, (1, LANES), 1)
    hi, mid, lo = _split3(jnp.where(lane < FOX_HEADS, c * LOG2E, 0.0))
    cpack = (hi.astype(F32) + pltpu.roll(mid.astype(F32), FOX_HEADS, 1)
             + pltpu.roll(lo.astype(F32), 2 * FOX_HEADS, 1)).astype(BF16)
    aug_q = _dot(cpack, pq_ref[...]) + oq_ref[...]
    aug_k = _dot(cpack, pk_ref[...]) + ok_ref[...]

    v = _dot(xn, w_ref[:, 2 * d:3 * d])
    gate_ref[...] = _dot(xn, wo_ref[...]).astype(gate_ref.dtype)

    bd = bd_ref[...]

    def head_norm(x, gain):
        ss = _dot((x * x).astype(BF16), bd)
        return x * lax.rsqrt(ss * (1.0 / FOX_HEAD_DIM) + EPS) * gain

    halves = (lane < FOX_HEAD_DIM, lane >= FOX_HEAD_DIM)
    for pair in range(FOX_PAIRS):
        sl = slice(pair * LANES, (pair + 1) * LANES)
        if pair % 2 == 0:
            sl2 = slice(pair * LANES, (pair + 2) * LANES)
            qn2 = head_norm(q[:, sl2], gq_ref[:, sl2]) * (FOX_HEAD_DIM ** -0.5 * LOG2E)
            kn2 = head_norm(k[:, sl2], gk_ref[:, sl2])
        inner = slice((pair % 2) * LANES, (pair % 2 + 1) * LANES)
        qn, kn = qn2[:, inner], kn2[:, inner]
        for e in range(2):
            hs = slice((2 * pair + e) * LANES, (2 * pair + e + 1) * LANES)
            qt_ref[0, hs, :] = jnp.where(halves[e], qn, aug_q[:, sl]).T.astype(BF16)
            kop_ref[:, hs] = jnp.where(halves[e], kn, aug_k[:, sl]).astype(BF16)
        vt = v[:, sl].T.astype(BF16)
        ones = jnp.ones((FOX_ONES_ROWS, vt.shape[1]), BF16)
        for e in range(2):
            r0 = (2 * pair + e) * FOX_VT_ROWS
            vt_ref[0, r0:r0 + FOX_HEAD_DIM, :] = vt[e * FOX_HEAD_DIM:(e + 1) * FOX_HEAD_DIM]
            vt_ref[0, r0 + FOX_HEAD_DIM:r0 + FOX_VT_ROWS, :] = ones


def _fox_proj_prep(x, g, w, w_o, w_f, b_f, g_q, g_k, batch, seq, ts):
    t, d = x.shape
    nblk = seq // ts
    n_op = FOX_HEADS * LANES
    bd = jnp.asarray(np.kron(np.eye(2 * LANES // FOX_HEAD_DIM), np.ones((FOX_HEAD_DIM, FOX_HEAD_DIM))), BF16)
    tri = jnp.asarray(np.tril(np.ones((ts, ts))), BF16)
    pq, pk, oq, ok = _fox_aug_matrices()
    bf = jnp.zeros((1, LANES), F32).at[0, :FOX_HEADS].set(b_f)
    gq = jnp.tile(g_q, FOX_HEADS).reshape(1, d)
    gk = jnp.tile(g_k, FOX_HEADS).reshape(1, d)
    rows = lambda b, s: (b * nblk + s, 0)
    const = lambda b, s: (0, 0)
    resident = lambda shape: pl.BlockSpec(shape, const, pipeline_mode=pl.Buffered(1))
    return pl.pallas_call(
        _fox_proj_prep_kernel,
        name="fox_proj_prep",
        grid=(batch, nblk),
        in_specs=[
            pl.BlockSpec((ts, d), rows),
            resident((1, d)),
            resident((d, 3 * d)),
            resident((d, d)),
            resident((d, LANES)),
            resident((1, LANES)),
            resident((1, d)),
            resident((1, d)),
            resident((2 * LANES, 2 * LANES)),
            resident((ts, ts)),
            resident((LANES, FOX_PAIRS * LANES)),
            resident((LANES, FOX_PAIRS * LANES)),
            resident((1, FOX_PAIRS * LANES)),
            resident((1, FOX_PAIRS * LANES)),
        ],
        out_specs=[
            pl.BlockSpec((1, n_op, ts), lambda b, s: (b, 0, s)),
            pl.BlockSpec((ts, n_op), rows),
            pl.BlockSpec((1, FOX_HEADS * FOX_VT_ROWS, ts), lambda b, s: (b, 0, s)),
            pl.BlockSpec((ts, d), rows),
        ],
        out_shape=[
            jax.ShapeDtypeStruct((batch, n_op, seq), BF16),
            jax.ShapeDtypeStruct((t, n_op), BF16),
            jax.ShapeDtypeStruct((batch, FOX_HEADS * FOX_VT_ROWS, seq), BF16),
            jax.ShapeDtypeStruct((t, d), BF16),
        ],
        scratch_shapes=[pltpu.VMEM((1, LANES), F32)],
        compiler_params=_cparams(("parallel", "arbitrary")),
    )(x, g.reshape(1, d), w, w_o, w_f, bf, gq, gk, bd, tri,
      jnp.asarray(pq, BF16), jnp.asarray(pk, BF16), jnp.asarray(oq), jnp.asarray(ok))


def _fox_attn_kernel(q_ref, k_ref, vt_ref, gate_ref, o_ref, acc_ref, m_ref, s_ref, *, tq):
    half = FOX_HEAD_DIM
    nq = q_ref.shape[2] // tq
    hand = (2, 3)

    def q_operands(qi):
        cols = pl.ds(pl.multiple_of(qi * tq, tq), tq)
        return [q_ref[0, e * LANES:(e + 1) * LANES, cols] for e in range(2)]

    def put_scores(qq, kj, slot, e):
        start = pl.multiple_of(kj * tq, tq)
        s_ref[slot, e] = _dot(k_ref[pl.ds(start, tq), e * LANES:(e + 1) * LANES], qq[e])

    def block(kj, slot, e, diagonal):
        start = pl.multiple_of(kj * tq, tq)
        vop = vt_ref[0, e * FOX_VT_ROWS:(e + 1) * FOX_VT_ROWS, pl.ds(start, tq)]
        s = s_ref[slot, e]
        if diagonal:
            key = lax.broadcasted_iota(jnp.int32, (tq, tq), 0)
            qry = lax.broadcasted_iota(jnp.int32, (tq, tq), 1)
            s = jnp.where(key <= qry, s, -jnp.inf)
        m_prev = m_ref[e]
        m_new = jnp.maximum(m_prev, jnp.max(s, axis=0, keepdims=True))
        alpha = jnp.exp2(m_prev - m_new)
        p = jnp.exp2(s - m_new).astype(BF16)
        acc_ref[e] = acc_ref[e] * alpha + _dot(vop, p)
        m_ref[e] = m_new

    def step(nxt, cur, diagonal=False):
        for e in range(2):
            put_scores(nxt[0], nxt[1], nxt[2], e)
            block(cur[0], cur[1], e, diagonal)

    def q_block(qi, parity):
        q_rows = pl.ds(pl.multiple_of(qi * tq, tq), tq)
        qq = q_operands(qi)
        following = (q_operands(jnp.minimum(qi + 1, nq - 1)), 0, hand[1 - parity])

        acc_ref[...] = jnp.zeros_like(acc_ref)
        m_ref[...] = jnp.full(m_ref.shape, NEG_BIG, F32)

        def middle(j0, count):
            for u in range(count):
                step((qq, j0 + u + 1, u % 2), (j0 + u, (u + 1) % 2))

        def general():
            step((qq, 1, 1), (0, hand[parity]))
            n_mid = qi - 1

            def unrolled_body(i, carry):
                middle(1 + ATTN_UNROLL * i, ATTN_UNROLL)
                return carry

            lax.fori_loop(0, n_mid // ATTN_UNROLL, unrolled_body, 0)
            j0 = 1 + (n_mid // ATTN_UNROLL) * ATTN_UNROLL
            for rem in range(1 - parity, ATTN_UNROLL, 2):
                if rem:
                    pl.when(n_mid % ATTN_UNROLL == rem)(functools.partial(middle, j0, rem))
            step(following, (qi, parity), True)

        if parity == 0:
            pl.when(qi == 0)(lambda: step(following, (0, hand[0]), True))
            pl.when(qi > 0)(general)
        else:
            general()

        outs = []
        for e in range(2):
            acc = acc_ref[e]
            denom = acc[half:half + 1]
            outs.append(acc[:half] * (1.0 / denom))
        o2 = jnp.concatenate(outs, axis=0).T
        o_ref[q_rows, :] = (o2 * _sigmoid(gate_ref[q_rows, :].astype(F32))).astype(o_ref.dtype)

    first = q_operands(0)
    for e in range(2):
        put_scores(first, 0, hand[0], e)

    def pair_body(jp, carry):
        q_block(2 * jp, 0)
        q_block(2 * jp + 1, 1)
        return carry

    lax.fori_loop(0, nq // 2, pair_body, 0)


def _fox_attn(qt, kop, vt, gate, batch, seq, tq):
    t = batch * seq
    return pl.pallas_call(
        functools.partial(_fox_attn_kernel, tq=tq),
        name="fox_attn",
        grid=(batch, FOX_PAIRS),
        in_specs=[
            pl.BlockSpec((1, 2 * LANES, seq), lambda b, hp: (b, hp, 0)),
            pl.BlockSpec((seq, 2 * LANES), lambda b, hp: (b, hp)),
            pl.BlockSpec((1, 2 * FOX_VT_ROWS, seq), lambda b, hp: (b, hp, 0)),
            pl.BlockSpec((seq, LANES), lambda b, hp: (b, hp)),
        ],
        out_specs=pl.BlockSpec((seq, LANES), lambda b, hp: (b, hp)),
        out_shape=jax.ShapeDtypeStruct((t, D_MODEL), BF16),
        scratch_shapes=[pltpu.VMEM((2, FOX_VT_ROWS, tq), F32), pltpu.VMEM((2, 1, tq), F32),
                        pltpu.VMEM((4, 2, tq, tq), F32)],
        compiler_params=_cparams(("parallel", "parallel")),
    )(qt, kop, vt, gate)


def kernel(x, l0_norm_mix, l0_w_in, l0_w_gk_up, l0_b_gk, l0_g_onorm, l0_w_out, l0_norm_ffn, l0_w_ff1, l0_w_ff2, l1_norm_mix, l1_w_in, l1_b_f, l1_g_q, l1_g_k, l1_w_out, l1_norm_ffn, l1_w_ff1, l1_w_ff2, final_norm):
    batch, seq, d = x.shape
    t = batch * seq
    xf = x.reshape(t, d)

    n_main = 2 * GLA_DK + 2 * GLA_DV
    w_main = l0_w_in[:, :n_main].astype(BF16)
    w_gate = jnp.zeros((d, LANES), F32).at[:, :GLA_GATE_RANK].set(l0_w_in[:, n_main:]).astype(BF16)
    w_up = jnp.zeros((LANES, GLA_DK), F32).at[:GLA_GATE_RANK].set(l0_w_gk_up)
    og = _gla_proj(xf, l0_norm_mix, w_main, w_gate, w_up, l0_b_gk, l0_g_onorm, batch, seq, ROW_TILE)
    xf = _mix_out_mlp(og, l0_w_out.astype(BF16), xf, l0_norm_ffn, l0_w_ff1.astype(BF16),
                      l0_w_ff2.astype(BF16), final_norm, ROW_TILE, FF_CHUNK, False)

    w_qkv = l1_w_in[:, :3 * d].astype(BF16)
    w_f = l1_w_in[:, 3 * d:3 * d + FOX_HEADS]
    w_g = l1_w_in[:, 3 * d + FOX_HEADS:].astype(BF16)
    w_fp = jnp.zeros((d, LANES), F32).at[:, :FOX_HEADS].set(w_f).astype(BF16)
    qt, kop, vt, gate = _fox_proj_prep(xf, l1_norm_mix, w_qkv, w_g, w_fp, l1_b_f, l1_g_q, l1_g_k, batch, seq, ROW_TILE)
    oa = _fox_attn(qt, kop, vt, gate, batch, seq, ATTN_BLOCK)
    xf = _mix_out_mlp(oa, l1_w_out.astype(BF16), xf, l1_norm_ffn, l1_w_ff1.astype(BF16),
                      l1_w_ff2.astype(BF16), final_norm, ROW_TILE, FF_CHUNK, True)
    return xf.reshape(batch, seq, d)
```

```python
import functools

import numpy as np
import jax
import jax.numpy as jnp
from jax import lax
from jax.experimental import pallas as pl
from jax.experimental.pallas import tpu as pltpu

F32 = jnp.float32
BF16 = jnp.bfloat16

EPS = 1e-6
D_MODEL = 1024

GLA_HEADS = 4
GLA_DK = 512
GLA_DV = 1024
GLA_HEAD_K = 128
GLA_HEAD_V = 256
GLA_GATE_RANK = 16
GLA_GATE_NORM = 16.0
GLA_BLOCK = 256
GLA_LEVELS = 8
GLA_STEP_HEADS = 4

FOX_HEADS = 16
FOX_HEAD_DIM = 64
FOX_PAIRS = FOX_HEADS // 2
FOX_ONES_ROWS = 16
FOX_VT_ROWS = FOX_HEAD_DIM + FOX_ONES_ROWS

LANES = 128
VMEM_LIMIT = 56 * 1024 * 1024

ROW_TILE = 512
FF_CHUNK = 1024
ATTN_BLOCK = 512
ATTN_UNROLL = 8
NEG_BIG = -1e30


def _cparams(sem):
    return pltpu.CompilerParams(dimension_semantics=sem, vmem_limit_bytes=VMEM_LIMIT)


def _dot(a, b):
    return jnp.dot(a, b, preferred_element_type=F32)


def _dot_nt(a, b):
    return lax.dot_general(a, b, (((1,), (1,)), ((), ())), preferred_element_type=F32)


def _dot_tn(a, b):
    return lax.dot_general(a, b, (((0,), (0,)), ((), ())), preferred_element_type=F32)


def _rms(x, g):
    ms = jnp.mean(x * x, axis=-1, keepdims=True)
    return x * lax.rsqrt(ms + EPS) * g


def _log_sigmoid(z):
    return jnp.minimum(z, 0.0) - jnp.log(1.0 + jnp.exp(-jnp.abs(z)))


def _sigmoid(z):
    return 0.5 * jnp.tanh(0.5 * z) + 0.5


def _mix_out_mlp_kernel(a_ref, wo_ref, x_ref, g_ref, w1_ref, w2_ref, gf_ref, o_ref, *, tf, final_norm):
    x1 = x_ref[...] + _dot(a_ref[...], wo_ref[...])
    xn = _rms(x1, g_ref[...]).astype(BF16)
    y = x1
    for c in range(w1_ref.shape[1] // tf):
        h = _dot(xn, w1_ref[:, c * tf:(c + 1) * tf])
        h = jnp.square(jnp.maximum(h, 0.0)).astype(BF16)
        y = y + _dot(h, w2_ref[c * tf:(c + 1) * tf, :])
    if final_norm:
        y = _rms(y, gf_ref[...])
    o_ref[...] = y


def _mix_out_mlp(a, w_out, x, g, w1, w2, gf, tm, tf, final_norm):
    t, d = x.shape
    ff = w1.shape[1]
    resident = lambda shape: pl.BlockSpec(shape, lambda i: (0, 0), pipeline_mode=pl.Buffered(1))
    return pl.pallas_call(
        functools.partial(_mix_out_mlp_kernel, tf=tf, final_norm=final_norm),
        name="mix_out_mlp",
        grid=(t // tm,),
        in_specs=[
            pl.BlockSpec((tm, d), lambda i: (i, 0)),
            resident((d, d)),
            pl.BlockSpec((tm, d), lambda i: (i, 0)),
            resident((1, d)),
            resident((d, ff)),
            resident((ff, d)),
            resident((1, d)),
        ],
        out_specs=pl.BlockSpec((tm, d), lambda i: (i, 0)),
        out_shape=jax.ShapeDtypeStruct((t, d), F32),
        compiler_params=_cparams(("parallel",)),
    )(a, w_out, x, g.reshape(1, d), w1, w2, gf.reshape(1, d))


def _gla_level_map(n):
    i = np.arange(n)[:, None]
    j = np.arange(n)[None, :]
    x = i ^ j
    lvl = np.zeros((n, n), np.int32)
    nz = x > 0
    lvl[nz] = np.floor(np.log2(x[nz])).astype(np.int32) + 1
    lvl = np.where(i > j, lvl, 0)
    lvl = np.where(i == j, GLA_LEVELS, lvl)
    return lvl.astype(np.int32)


def _gla_block(q, k, v, r, gl, wup, bgk, gon, lvl, st_ref, st_base):
    L = GLA_BLOCK
    HK, HV = GLA_HEAD_K, GLA_HEAD_V
    heads = range(GLA_STEP_HEADS)

    z = _dot(gl, wup) + bgk
    g = _log_sigmoid(z) * (1.0 / GLA_GATE_NORM)

    qs = q * (HK ** -0.5)
    H2 = L // 2
    SUB = 8
    row = lax.broadcasted_iota(jnp.int32, qs.shape, 0)

    def halves(x, h):
        lo = [x[r:r + h] for r in range(0, L, 2 * h)]
        up = [x[r + h:r + 2 * h] for r in range(0, L, 2 * h)]
        return lo, up

    def interleave(lo, up):
        return jnp.concatenate([piece for pair in zip(lo, up) for piece in pair], axis=0)

    def diag_products(xq, xk):
        return [[_dot_nt(xq[i * H2:(i + 1) * H2, hh * HK:(hh + 1) * HK],
                         xk[i * H2:(i + 1) * H2, hh * HK:(hh + 1) * HK]) for i in range(2)] for hh in heads]

    def level_update(a_diag, t, xs):
        xs = xs.astype(BF16)
        prods = diag_products(xs, xs)
        return [[jnp.where(lvl == t + 1, prods[hh][i], a_diag[hh][i]) for i in range(2)] for hh in heads]

    prods = diag_products(qs.astype(BF16), k.astype(BF16))
    a_diag = [[jnp.where(lvl == GLA_LEVELS, prods[hh][i], 0.0) for i in range(2)] for hh in heads]
    p = g
    tot = g
    a_off = None
    for t in range(GLA_LEVELS):
        h = 1 << t
        if h < SUB:
            upper = (row & h) != 0
            e = jnp.where(upper, p, tot - p)
            a_diag = level_update(a_diag, t, jnp.where(upper, qs, k) * jnp.exp(e))
            t_lo = pltpu.roll(tot, h, 0)
            t_hi = pltpu.roll(tot, L - h, 0)
            p = p + jnp.where(upper, t_lo, 0.0)
            tot = tot + jnp.where(upper, t_lo, t_hi)
        else:
            p_lo, p_up = halves(p, h)
            t_lo, t_up = halves(tot, h)
            q_up = halves(qs, h)[1]
            k_lo = halves(k, h)[0]
            x_lo = [kk * jnp.exp(tl - pl_) for kk, tl, pl_ in zip(k_lo, t_lo, p_lo)]
            x_up = [qq * jnp.exp(pu) for qq, pu in zip(q_up, p_up)]
            if 2 * h < L:
                a_diag = level_update(a_diag, t, interleave(x_lo, x_up))
            else:
                xu, xl = x_up[0].astype(BF16), x_lo[0].astype(BF16)
                a_off = [_dot_nt(xu[:, hh * HK:(hh + 1) * HK], xl[:, hh * HK:(hh + 1) * HK]) for hh in heads]
            t_new = [tl + tu for tl, tu in zip(t_lo, t_up)]
            p = interleave(p_lo, [pu + tl for pu, tl in zip(p_up, t_lo)])
            tot = interleave(t_new, t_new)

    qe = (qs * jnp.exp(p)).astype(BF16)
    kd = (k * jnp.exp(tot - p)).astype(BF16)
    decay = jnp.exp(tot[0:1, :])
    outs = []
    for hh in heads:
        ks = slice(hh * HK, (hh + 1) * HK)
        vs = slice(hh * HV, (hh + 1) * HV)
        st = st_ref[st_base + hh]
        vh = v[:, vs]
        a_top = a_diag[hh][0].astype(BF16)
        a_bot = jnp.concatenate([a_off[hh], a_diag[hh][1]], axis=1).astype(BF16)
        o_intra = jnp.concatenate([_dot(a_top, vh[:H2]), _dot(a_bot, vh)], axis=0)
        o = _dot_nt(qe[:, ks], st.astype(BF16)) + o_intra
        st_ref[st_base + hh] = st * decay[:, ks] + _dot_tn(vh, kd[:, ks])
        rh = r[:, vs]
        outs.append((_rms(o, gon) * (rh * _sigmoid(rh))).astype(BF16))
    return jnp.concatenate(outs, axis=1)


def _gla_proj_kernel(x_ref, g_ref, w_ref, wg_ref, wup_ref, bgk_ref, gon_ref, lvl_ref, o_ref, st_ref):
    L = GLA_BLOCK
    wk = GLA_STEP_HEADS * GLA_HEAD_K
    wv = GLA_STEP_HEADS * GLA_HEAD_V

    @pl.when(pl.program_id(1) == 0)
    def _():
        st_ref[...] = jnp.zeros_like(st_ref)

    xn = _rms(x_ref[...], g_ref[...]).astype(BF16)
    q = _dot(xn, w_ref[:, 0:GLA_DK])
    k = _dot(xn, w_ref[:, GLA_DK:2 * GLA_DK])
    gl = _dot(xn, wg_ref[...])
    v = _dot(xn, w_ref[:, 2 * GLA_DK:2 * GLA_DK + GLA_DV]).astype(BF16)
    r = _dot(xn, w_ref[:, 2 * GLA_DK + GLA_DV:])
    gon = gon_ref[...]
    lvl = lvl_ref[...]
    for blk in range(x_ref.shape[0] // L):
        rows = slice(blk * L, (blk + 1) * L)
        for hp in range(GLA_HEADS // GLA_STEP_HEADS):
            ks = slice(hp * wk, (hp + 1) * wk)
            vs = slice(hp * wv, (hp + 1) * wv)
            o_ref[rows, vs] = _gla_block(q[rows, ks], k[rows, ks], v[rows, vs], r[rows, vs], gl[rows],
                                         wup_ref[:, ks], bgk_ref[:, ks], gon, lvl, st_ref, hp * GLA_STEP_HEADS)


def _gla_proj(x, g, w, w_gate, wup, bgk, gon, batch, seq, ts):
    t, d = x.shape
    nblk = seq // ts
    lvl = jnp.asarray(_gla_level_map(GLA_BLOCK // 2))
    const = lambda b, s: (0, 0)
    resident = lambda shape: pl.BlockSpec(shape, const, pipeline_mode=pl.Buffered(1))
    return pl.pallas_call(
        _gla_proj_kernel,
        name="gla_proj",
        grid=(batch, nblk),
        in_specs=[
            pl.BlockSpec((ts, d), lambda b, s: (b * nblk + s, 0)),
            resident((1, d)),
            resident(w.shape),
            resident((d, LANES)),
            resident((LANES, GLA_DK)),
            resident((1, GLA_DK)),
            resident((1, GLA_HEAD_V)),
            resident((GLA_BLOCK // 2, GLA_BLOCK // 2)),
        ],
        out_specs=pl.BlockSpec((ts, GLA_DV), lambda b, s: (b * nblk + s, 0)),
        out_shape=jax.ShapeDtypeStruct((t, GLA_DV), BF16),
        scratch_shapes=[pltpu.VMEM((GLA_HEADS, GLA_HEAD_V, GLA_HEAD_K), F32)],
        compiler_params=_cparams(("parallel", "arbitrary")),
    )(x, g.reshape(1, d), w, w_gate, wup, bgk.reshape(1, GLA_DK), gon.reshape(1, GLA_HEAD_V), lvl)


LOG2E = 1.4426950408889634
C_PIECES = 3


def _split3(x):
    hi = x.astype(BF16)
    r = x - hi.astype(F32)
    mid = r.astype(BF16)
    lo = (r - mid.astype(F32)).astype(BF16)
    return hi, mid, lo


def _fox_aug_matrices():
    n = FOX_PAIRS * LANES
    pq = np.zeros((LANES, n), np.float32)
    pk = np.zeros((LANES, n), np.float32)
    oq = np.zeros((1, n), np.float32)
    ok = np.zeros((1, n), np.float32)
    for h in range(FOX_HEADS):
        o = (h // 2) * LANES + (FOX_HEAD_DIM if h % 2 == 0 else 0)
        for t in range(C_PIECES):
            pq[FOX_HEADS * t + h, o + t] = 1.0
            ok[0, o + t] = 1.0
            pk[FOX_HEADS * t + h, o + C_PIECES + t] = -1.0
            oq[0, o + C_PIECES + t] = 1.0
    return pq, pk, oq, ok


def _fox_proj_prep_kernel(x_ref, g_ref, w_ref, wo_ref, wf_ref, bf_ref, gq_ref, gk_ref, bd_ref, tri_ref,
                          pq_ref, pk_ref, oq_ref, ok_ref, qt_ref, kop_ref, vt_ref, gate_ref, carry_ref):
    d = D_MODEL

    @pl.when(pl.program_id(1) == 0)
    def _():
        carry_ref[...] = jnp.zeros_like(carry_ref)

    xn = _rms(x_ref[...], g_ref[...]).astype(BF16)
    q = _dot(xn, w_ref[:, 0:d])
    k = _dot(xn, w_ref[:, d:2 * d])

    lf = _log_sigmoid(_dot(xn, wf_ref[...]) + bf_ref[...])
    hi, mid, lo = _split3(lf)
    tri = tri_ref[...]
    c_hm = _dot(tri, jnp.concatenate([hi, mid], axis=1))
    c = carry_ref[...] + (c_hm[:, :LANES] + c_hm[:, LANES:] + _dot(tri, lo))
    carry_ref[...] = c[c.shape[0] - 1:, :]

    lane = lax.broadcasted_iota(jnp.int32, (1, LANES), 1)
    hi, mid, lo = _split3(jnp.where(lane < FOX_HEADS, c * LOG2E, 0.0))
    cpack = (hi.astype(F32) + pltpu.roll(mid.astype(F32), FOX_HEADS, 1)
             + pltpu.roll(lo.astype(F32), 2 * FOX_HEADS, 1)).astype(BF16)
    aug_q = _dot(cpack, pq_ref[...]) + oq_ref[...]
    aug_k = _dot(cpack, pk_ref[...]) + ok_ref[...]

    bd = bd_ref[...]

    def head_norm(x, gain):
        ss = _dot((x * x).astype(BF16), bd)
        return x * lax.rsqrt(ss * (1.0 / FOX_HEAD_DIM) + EPS) * gain

    halves = (lane < FOX_HEAD_DIM, lane >= FOX_HEAD_DIM)
    for pair in range(FOX_PAIRS):
        sl = slice(pair * LANES, (pair + 1) * LANES)
        if pair % 2 == 0:
            sl2 = slice(pair * LANES, (pair + 2) * LANES)
            qn2 = head_norm(q[:, sl2], gq_ref[:, sl2]) * (FOX_HEAD_DIM ** -0.5 * LOG2E)
            kn2 = head_norm(k[:, sl2], gk_ref[:, sl2])
            v2 = _dot(xn, w_ref[:, 2 * d + pair * LANES:2 * d + (pair + 2) * LANES])
            gate_ref[:, sl2] = _dot(xn, wo_ref[:, sl2]).astype(gate_ref.dtype)
        inner = slice((pair % 2) * LANES, (pair % 2 + 1) * LANES)
        qn, kn = qn2[:, inner], kn2[:, inner]
        for e in range(2):
            hs = slice((2 * pair + e) * LANES, (2 * pair + e + 1) * LANES)
            qt_ref[0, hs, :] = jnp.where(halves[e], qn, aug_q[:, sl]).T.astype(BF16)
            kop_ref[:, hs] = jnp.where(halves[e], kn, aug_k[:, sl]).astype(BF16)
        vt = v2[:, inner].T.astype(BF16)
        ones = jnp.ones((FOX_ONES_ROWS, vt.shape[1]), BF16)
        for e in range(2):
            r0 = (2 * pair + e) * FOX_VT_ROWS
            vt_ref[0, r0:r0 + FOX_HEAD_DIM, :] = vt[e * FOX_HEAD_DIM:(e + 1) * FOX_HEAD_DIM]
            vt_ref[0, r0 + FOX_HEAD_DIM:r0 + FOX_VT_ROWS, :] = ones


def _fox_proj_prep(x, g, w, w_o, w_f, b_f, g_q, g_k, batch, seq, ts):
    t, d = x.shape
    nblk = seq // ts
    n_op = FOX_HEADS * LANES
    bd = jnp.asarray(np.kron(np.eye(2 * LANES // FOX_HEAD_DIM), np.ones((FOX_HEAD_DIM, FOX_HEAD_DIM))), BF16)
    tri = jnp.asarray(np.tril(np.ones((ts, ts))), BF16)
    pq, pk, oq, ok = _fox_aug_matrices()
    bf = jnp.zeros((1, LANES), F32).at[0, :FOX_HEADS].set(b_f)
    gq = jnp.tile(g_q, FOX_HEADS).reshape(1, d)
    gk = jnp.tile(g_k, FOX_HEADS).reshape(1, d)
    rows = lambda b, s: (b * nblk + s, 0)
    const = lambda b, s: (0, 0)
    resident = lambda shape: pl.BlockSpec(shape, const, pipeline_mode=pl.Buffered(1))
    return pl.pallas_call(
        _fox_proj_prep_kernel,
        name="fox_proj_prep",
        grid=(batch, nblk),
        in_specs=[
            pl.BlockSpec((ts, d), rows),
            resident((1, d)),
            resident((d, 3 * d)),
            resident((d, d)),
            resident((d, LANES)),
            resident((1, LANES)),
            resident((1, d)),
            resident((1, d)),
            resident((2 * LANES, 2 * LANES)),
            resident((ts, ts)),
            resident((LANES, FOX_PAIRS * LANES)),
            resident((LANES, FOX_PAIRS * LANES)),
            resident((1, FOX_PAIRS * LANES)),
            resident((1, FOX_PAIRS * LANES)),
        ],
        out_specs=[
            pl.BlockSpec((1, n_op, ts), lambda b, s: (b, 0, s)),
            pl.BlockSpec((ts, n_op), rows),
            pl.BlockSpec((1, FOX_HEADS * FOX_VT_ROWS, ts), lambda b, s: (b, 0, s)),
            pl.BlockSpec((ts, d), rows),
        ],
        out_shape=[
            jax.ShapeDtypeStruct((batch, n_op, seq), BF16),
            jax.ShapeDtypeStruct((t, n_op), BF16),
            jax.ShapeDtypeStruct((batch, FOX_HEADS * FOX_VT_ROWS, seq), BF16),
            jax.ShapeDtypeStruct((t, d), BF16),
        ],
        scratch_shapes=[pltpu.VMEM((1, LANES), F32)],
        compiler_params=_cparams(("parallel", "arbitrary")),
    )(x, g.reshape(1, d), w, w_o, w_f, bf, gq, gk, bd, tri,
      jnp.asarray(pq, BF16), jnp.asarray(pk, BF16), jnp.asarray(oq), jnp.asarray(ok))


def _fox_attn_kernel(q_ref, k_ref, vt_ref, gate_ref, o_ref, acc_ref, m_ref, s_ref, *, tq):
    half = FOX_HEAD_DIM
    nq = q_ref.shape[2] // tq
    hand = (2, 3)

    def q_operands(qi):
        cols = pl.ds(pl.multiple_of(qi * tq, tq), tq)
        return [q_ref[0, e * LANES:(e + 1) * LANES, cols] for e in range(2)]

    def put_scores(qq, kj, slot, e):
        start = pl.multiple_of(kj * tq, tq)
        s_ref[slot, e] = _dot(k_ref[pl.ds(start, tq), e * LANES:(e + 1) * LANES], qq[e])

    def block(kj, slot, e, diagonal):
        start = pl.multiple_of(kj * tq, tq)
        vop = vt_ref[0, e * FOX_VT_ROWS:(e + 1) * FOX_VT_ROWS, pl.ds(start, tq)]
        s = s_ref[slot, e]
        if diagonal:
            key = lax.broadcasted_iota(jnp.int32, (tq, tq), 0)
            qry = lax.broadcasted_iota(jnp.int32, (tq, tq), 1)
            s = jnp.where(key <= qry, s, -jnp.inf)
        m_prev = m_ref[e]
        m_new = jnp.maximum(m_prev, jnp.max(s, axis=0, keepdims=True))
        alpha = jnp.exp2(m_prev - m_new)
        p = jnp.exp2(s - m_new).astype(BF16)
        acc_ref[e] = acc_ref[e] * alpha + _dot(vop, p)
        m_ref[e] = m_new

    def step(nxt, cur, diagonal=False):
        for e in range(2):
            put_scores(nxt[0], nxt[1], nxt[2], e)
            block(cur[0], cur[1], e, diagonal)

    def q_block(qi, parity):
        q_rows = pl.ds(pl.multiple_of(qi * tq, tq), tq)
        qq = q_operands(qi)
        following = (q_operands(jnp.minimum(qi + 1, nq - 1)), 0, hand[1 - parity])

        acc_ref[...] = jnp.zeros_like(acc_ref)
        m_ref[...] = jnp.full(m_ref.shape, NEG_BIG, F32)

        def middle(j0, count):
            for u in range(count):
                step((qq, j0 + u + 1, u % 2), (j0 + u, (u + 1) % 2))

        def general():
            step((qq, 1, 1), (0, hand[parity]))
            n_mid = qi - 1

            def unrolled_body(i, carry):
                middle(1 + ATTN_UNROLL * i, ATTN_UNROLL)
                return carry

            lax.fori_loop(0, n_mid // ATTN_UNROLL, unrolled_body, 0)
            j0 = 1 + (n_mid // ATTN_UNROLL) * ATTN_UNROLL
            for rem in range(1 - parity, ATTN_UNROLL, 2):
                if rem:
                    pl.when(n_mid % ATTN_UNROLL == rem)(functools.partial(middle, j0, rem))
            step(following, (qi, parity), True)

        if parity == 0:
            pl.when(qi == 0)(lambda: step(following, (0, hand[0]), True))
            pl.when(qi > 0)(general)
        else:
            general()

        outs = []
        for e in range(2):
            acc = acc_ref[e]
            denom = acc[half:half + 1]
            outs.append(acc[:half] * (1.0 / denom))
        o2 = jnp.concatenate(outs, axis=0).T
        o_ref[q_rows, :] = (o2 * _sigmoid(gate_ref[q_rows, :].astype(F32))).astype(o_ref.dtype)

    first = q_operands(0)
    for e in range(2):
        put_scores(first, 0, hand[0], e)

    def pair_body(jp, carry):
        q_block(2 * jp, 0)
        q_block(2 * jp + 1, 1)
        return carry

    lax.fori_loop(0, nq // 2, pair_body, 0)


def _fox_attn(qt, kop, vt, gate, batch, seq, tq):
    t = batch * seq
    return pl.pallas_call(
        functools.partial(_fox_attn_kernel, tq=tq),
        name="fox_attn",
        grid=(batch, FOX_PAIRS),
        in_specs=[
            pl.BlockSpec((1, 2 * LANES, seq), lambda b, hp: (b, hp, 0)),
            pl.BlockSpec((seq, 2 * LANES), lambda b, hp: (b, hp)),
            pl.BlockSpec((1, 2 * FOX_VT_ROWS, seq), lambda b, hp: (b, hp, 0)),
            pl.BlockSpec((seq, LANES), lambda b, hp: (b, hp)),
        ],
        out_specs=pl.BlockSpec((seq, LANES), lambda b, hp: (b, hp)),
        out_shape=jax.ShapeDtypeStruct((t, D_MODEL), BF16),
        scratch_shapes=[pltpu.VMEM((2, FOX_VT_ROWS, tq), F32), pltpu.VMEM((2, 1, tq), F32),
                        pltpu.VMEM((4, 2, tq, tq), F32)],
        compiler_params=_cparams(("parallel", "parallel")),
    )(qt, kop, vt, gate)


def kernel(x, l0_norm_mix, l0_w_in, l0_w_gk_up, l0_b_gk, l0_g_onorm, l0_w_out, l0_norm_ffn, l0_w_ff1, l0_w_ff2, l1_norm_mix, l1_w_in, l1_b_f, l1_g_q, l1_g_k, l1_w_out, l1_norm_ffn, l1_w_ff1, l1_w_ff2, final_norm):
    batch, seq, d = x.shape
    t = batch * seq
    xf = x.reshape(t, d)

    n_main = 2 * GLA_DK + 2 * GLA_DV
    w_main = l0_w_in[:, :n_main].astype(BF16)
    w_gate = jnp.zeros((d, LANES), F32).at[:, :GLA_GATE_RANK].set(l0_w_in[:, n_main:]).astype(BF16)
    w_up = jnp.zeros((LANES, GLA_DK), F32).at[:GLA_GATE_RANK].set(l0_w_gk_up)
    og = _gla_proj(xf, l0_norm_mix, w_main, w_gate, w_up, l0_b_gk, l0_g_onorm, batch, seq, ROW_TILE)
    xf = _mix_out_mlp(og, l0_w_out.astype(BF16), xf, l0_norm_ffn, l0_w_ff1.astype(BF16),
                      l0_w_ff2.astype(BF16), final_norm, ROW_TILE, FF_CHUNK, False)

    w_qkv = l1_w_in[:, :3 * d].astype(BF16)
    w_f = l1_w_in[:, 3 * d:3 * d + FOX_HEADS]
    w_g = l1_w_in[:, 3 * d + FOX_HEADS:].astype(BF16)
    w_fp = jnp.zeros((d, LANES), F32).at[:, :FOX_HEADS].set(w_f).astype(BF16)
    qt, kop, vt, gate = _fox_proj_prep(xf, l1_norm_mix, w_qkv, w_g, w_fp, l1_b_f, l1_g_q, l1_g_k, batch, seq, ROW_TILE)
    oa = _fox_attn(qt, kop, vt, gate, batch, seq, ATTN_BLOCK)
    xf = _mix_out_mlp(oa, l1_w_out.astype(BF16), xf, l1_norm_ffn, l1_w_ff1.astype(BF16),
                      l1_w_ff2.astype(BF16), final_norm, ROW_TILE, FF_CHUNK, True)
    return xf.reshape(batch, seq, d)
```

```python
import functools

import numpy as np
import jax
import jax.numpy as jnp
from jax import lax
from jax.experimental import pallas as pl
from jax.experimental.pallas import tpu as pltpu

F32 = jnp.float32
BF16 = jnp.bfloat16

EPS = 1e-6
D_MODEL = 1024

GLA_HEADS = 4
GLA_DK = 512
GLA_DV = 1024
GLA_HEAD_K = 128
GLA_HEAD_V = 256
GLA_GATE_RANK = 16
GLA_GATE_NORM = 16.0
GLA_BLOCK = 256
GLA_LEVELS = 8
GLA_STEP_HEADS = 4

FOX_HEADS = 16
FOX_HEAD_DIM = 64
FOX_PAIRS = FOX_HEADS // 2
FOX_ONES_ROWS = 16
FOX_VT_ROWS = FOX_HEAD_DIM + FOX_ONES_ROWS

LANES = 128
VMEM_LIMIT = 56 * 1024 * 1024

ROW_TILE = 512
FF_CHUNK = 1024
ATTN_BLOCK = 512
ATTN_UNROLL = 8
NEG_BIG = -1e30


def _cparams(sem):
    return pltpu.CompilerParams(dimension_semantics=sem, vmem_limit_bytes=VMEM_LIMIT)


def _dot(a, b):
    return jnp.dot(a, b, preferred_element_type=F32)


def _dot_nt(a, b):
    return lax.dot_general(a, b, (((1,), (1,)), ((), ())), preferred_element_type=F32)


def _dot_tn(a, b):
    return lax.dot_general(a, b, (((0,), (0,)), ((), ())), preferred_element_type=F32)


def _rms(x, g):
    ms = jnp.mean(x * x, axis=-1, keepdims=True)
    return x * lax.rsqrt(ms + EPS) * g


def _log_sigmoid(z):
    return jnp.minimum(z, 0.0) - jnp.log(1.0 + jnp.exp(-jnp.abs(z)))


def _sigmoid(z):
    return 0.5 * jnp.tanh(0.5 * z) + 0.5


def _mix_out_mlp_kernel(a_ref, wo_ref, x_ref, g_ref, w1_ref, w2_ref, gf_ref, o_ref, *, tf, final_norm):
    x1 = x_ref[...] + _dot(a_ref[...], wo_ref[...])
    xn = _rms(x1, g_ref[...]).astype(BF16)
    y = x1
    for c in range(w1_ref.shape[1] // tf):
        h = _dot(xn, w1_ref[:, c * tf:(c + 1) * tf])
        h = jnp.square(jnp.maximum(h, 0.0)).astype(BF16)
        y = y + _dot(h, w2_ref[c * tf:(c + 1) * tf, :])
    if final_norm:
        y = _rms(y, gf_ref[...])
    o_ref[...] = y


def _mix_out_mlp(a, w_out, x, g, w1, w2, gf, tm, tf, final_norm):
    t, d = x.shape
    ff = w1.shape[1]
    resident = lambda shape: pl.BlockSpec(shape, lambda i: (0, 0), pipeline_mode=pl.Buffered(1))
    return pl.pallas_call(
        functools.partial(_mix_out_mlp_kernel, tf=tf, final_norm=final_norm),
        name="mix_out_mlp",
        grid=(t // tm,),
        in_specs=[
            pl.BlockSpec((tm, d), lambda i: (i, 0)),
            resident((d, d)),
            pl.BlockSpec((tm, d), lambda i: (i, 0)),
            resident((1, d)),
            resident((d, ff)),
            resident((ff, d)),
            resident((1, d)),
        ],
        out_specs=pl.BlockSpec((tm, d), lambda i: (i, 0)),
        out_shape=jax.ShapeDtypeStruct((t, d), F32),
        compiler_params=_cparams(("parallel",)),
    )(a, w_out, x, g.reshape(1, d), w1, w2, gf.reshape(1, d))


def _gla_level_map(n):
    i = np.arange(n)[:, None]
    j = np.arange(n)[None, :]
    x = i ^ j
    lvl = np.zeros((n, n), np.int32)
    nz = x > 0
    lvl[nz] = np.floor(np.log2(x[nz])).astype(np.int32) + 1
    lvl = np.where(i > j, lvl, 0)
    lvl = np.where(i == j, GLA_LEVELS, lvl)
    return lvl.astype(np.int32)


def _gla_block(q, k, v, r, gl, wup, bgk, gon, lvl, st_ref, st_base):
    L = GLA_BLOCK
    HK, HV = GLA_HEAD_K, GLA_HEAD_V
    heads = range(GLA_STEP_HEADS)

    z = _dot(gl, wup) + bgk
    g = _log_sigmoid(z) * (1.0 / GLA_GATE_NORM)

    qs = q * (HK ** -0.5)
    H2 = L // 2
    SUB = 8
    row = lax.broadcasted_iota(jnp.int32, qs.shape, 0)

    def halves(x, h):
        lo = [x[r:r + h] for r in range(0, L, 2 * h)]
        up = [x[r + h:r + 2 * h] for r in range(0, L, 2 * h)]
        return lo, up

    def interleave(lo, up):
        return jnp.concatenate([piece for pair in zip(lo, up) for piece in pair], axis=0)

    def diag_products(xq, xk):
        return [[_dot_nt(xq[i * H2:(i + 1) * H2, hh * HK:(hh + 1) * HK],
                         xk[i * H2:(i + 1) * H2, hh * HK:(hh + 1) * HK]) for i in range(2)] for hh in heads]

    def level_update(a_diag, t, xs):
        xs = xs.astype(BF16)
        prods = diag_products(xs, xs)
        return [[jnp.where(lvl == t + 1, prods[hh][i], a_diag[hh][i]) for i in range(2)] for hh in heads]

    prods = diag_products(qs.astype(BF16), k.astype(BF16))
    a_diag = [[jnp.where(lvl == GLA_LEVELS, prods[hh][i], 0.0) for i in range(2)] for hh in heads]
    p = g
    tot = g
    a_off = None
    for t in range(GLA_LEVELS):
        h = 1 << t
        if h < SUB:
            upper = (row & h) != 0
            e = jnp.where(upper, p, tot - p)
            a_diag = level_update(a_diag, t, jnp.where(upper, qs, k) * jnp.exp(e))
            t_lo = pltpu.roll(tot, h, 0)
            t_hi = pltpu.roll(tot, L - h, 0)
            p = p + jnp.where(upper, t_lo, 0.0)
            tot = tot + jnp.where(upper, t_lo, t_hi)
        else:
            p_lo, p_up = halves(p, h)
            t_lo, t_up = halves(tot, h)
            q_up = halves(qs, h)[1]
            k_lo = halves(k, h)[0]
            x_lo = [kk * jnp.exp(tl - pl_) for kk, tl, pl_ in zip(k_lo, t_lo, p_lo)]
            x_up = [qq * jnp.exp(pu) for qq, pu in zip(q_up, p_up)]
            if 2 * h < L:
                a_diag = level_update(a_diag, t, interleave(x_lo, x_up))
            else:
                xu, xl = x_up[0].astype(BF16), x_lo[0].astype(BF16)
                a_off = [_dot_nt(xu[:, hh * HK:(hh + 1) * HK], xl[:, hh * HK:(hh + 1) * HK]) for hh in heads]
            t_new = [tl + tu for tl, tu in zip(t_lo, t_up)]
            p = interleave(p_lo, [pu + tl for pu, tl in zip(p_up, t_lo)])
            tot = interleave(t_new, t_new)

    qe = (qs * jnp.exp(p)).astype(BF16)
    kd = (k * jnp.exp(tot - p)).astype(BF16)
    decay = jnp.exp(tot[0:1, :])
    outs = []
    for hh in heads:
        ks = slice(hh * HK, (hh + 1) * HK)
        vs = slice(hh * HV, (hh + 1) * HV)
        st = st_ref[st_base + hh]
        vh = v[:, vs]
        a_top = a_diag[hh][0].astype(BF16)
        a_bot = jnp.concatenate([a_off[hh], a_diag[hh][1]], axis=1).astype(BF16)
        o_intra = jnp.concatenate([_dot(a_top, vh[:H2]), _dot(a_bot, vh)], axis=0)
        o = _dot_nt(qe[:, ks], st.astype(BF16)) + o_intra
        st_ref[st_base + hh] = st * decay[:, ks] + _dot_tn(vh, kd[:, ks])
        rh = r[:, vs]
        outs.append((_rms(o, gon) * (rh * _sigmoid(rh))).astype(BF16))
    return jnp.concatenate(outs, axis=1)


def _gla_proj_kernel(x_ref, g_ref, w_ref, wg_ref, wup_ref, bgk_ref, gon_ref, lvl_ref, o_ref, st_ref):
    L = GLA_BLOCK
    wk = GLA_STEP_HEADS * GLA_HEAD_K
    wv = GLA_STEP_HEADS * GLA_HEAD_V

    @pl.when(pl.program_id(1) == 0)
    def _():
        st_ref[...] = jnp.zeros_like(st_ref)

    xn = _rms(x_ref[...], g_ref[...]).astype(BF16)
    q = _dot(xn, w_ref[:, 0:GLA_DK])
    k = _dot(xn, w_ref[:, GLA_DK:2 * GLA_DK])
    gl = _dot(xn, wg_ref[...])
    v = _dot(xn, w_ref[:, 2 * GLA_DK:2 * GLA_DK + GLA_DV]).astype(BF16)
    r = _dot(xn, w_ref[:, 2 * GLA_DK + GLA_DV:])
    gon = gon_ref[...]
    lvl = lvl_ref[...]
    for blk in range(x_ref.shape[0] // L):
        rows = slice(blk * L, (blk + 1) * L)
        for hp in range(GLA_HEADS // GLA_STEP_HEADS):
            ks = slice(hp * wk, (hp + 1) * wk)
            vs = slice(hp * wv, (hp + 1) * wv)
            o_ref[rows, vs] = _gla_block(q[rows, ks], k[rows, ks], v[rows, vs], r[rows, vs], gl[rows],
                                         wup_ref[:, ks], bgk_ref[:, ks], gon, lvl, st_ref, hp * GLA_STEP_HEADS)


def _gla_proj(x, g, w, w_gate, wup, bgk, gon, batch, seq, ts):
    t, d = x.shape
    nblk = seq // ts
    lvl = jnp.asarray(_gla_level_map(GLA_BLOCK // 2))
    const = lambda b, s: (0, 0)
    resident = lambda shape: pl.BlockSpec(shape, const, pipeline_mode=pl.Buffered(1))
    return pl.pallas_call(
        _gla_proj_kernel,
        name="gla_proj",
        grid=(batch, nblk),
        in_specs=[
            pl.BlockSpec((ts, d), lambda b, s: (b * nblk + s, 0)),
            resident((1, d)),
            resident(w.shape),
            resident((d, LANES)),
            resident((LANES, GLA_DK)),
            resident((1, GLA_DK)),
            resident((1, GLA_HEAD_V)),
            resident((GLA_BLOCK // 2, GLA_BLOCK // 2)),
        ],
        out_specs=pl.BlockSpec((ts, GLA_DV), lambda b, s: (b * nblk + s, 0)),
        out_shape=jax.ShapeDtypeStruct((t, GLA_DV), BF16),
        scratch_shapes=[pltpu.VMEM((GLA_HEADS, GLA_HEAD_V, GLA_HEAD_K), F32)],
        compiler_params=_cparams(("parallel", "arbitrary")),
    )(x, g.reshape(1, d), w, w_gate, wup, bgk.reshape(1, GLA_DK), gon.reshape(1, GLA_HEAD_V), lvl)


LOG2E = 1.4426950408889634
C_PIECES = 3


def _split3(x):
    hi = x.astype(BF16)
    r = x - hi.astype(F32)
    mid = r.astype(BF16)
    lo = (r - mid.astype(F32)).astype(BF16)
    return hi, mid, lo


def _fox_aug_matrices():
    n = FOX_PAIRS * LANES
    pq = np.zeros((LANES, n), np.float32)
    pk = np.zeros((LANES, n), np.float32)
    oq = np.zeros((1, n), np.float32)
    ok = np.zeros((1, n), np.float32)
    for h in range(FOX_HEADS):
        o = (h // 2) * LANES + (FOX_HEAD_DIM if h % 2 == 0 else 0)
        for t in range(C_PIECES):
            pq[FOX_HEADS * t + h, o + t] = 1.0
            ok[0, o + t] = 1.0
            pk[FOX_HEADS * t + h, o + C_PIECES + t] = -1.0
            oq[0, o + C_PIECES + t] = 1.0
    return pq, pk, oq, ok


def _fox_proj_prep_kernel(x_ref, g_ref, w_ref, wo_ref, wf_ref, bf_ref, gq_ref, gk_ref, bd_ref, tri_ref,
                          pq_ref, pk_ref, oq_ref, ok_ref, qt_ref, kop_ref, vt_ref, gate_ref, carry_ref):
    d = D_MODEL

    @pl.when(pl.program_id(1) == 0)
    def _():
        carry_ref[...] = jnp.zeros_like(carry_ref)

    xn = _rms(x_ref[...], g_ref[...]).astype(BF16)
    q = _dot(xn, w_ref[:, 0:d])
    k = _dot(xn, w_ref[:, d:2 * d])

    lf = _log_sigmoid(_dot(xn, wf_ref[...]) + bf_ref[...])
    hi, mid, lo = _split3(lf)
    tri = tri_ref[...]
    c_hm = _dot(tri, jnp.concatenate([hi, mid], axis=1))
    c = carry_ref[...] + (c_hm[:, :LANES] + c_hm[:, LANES:] + _dot(tri, lo))
    carry_ref[...] = c[c.shape[0] - 1:, :]

    lane = lax.broadcasted_iota(jnp.int32, (1, LANES), 1)
    hi, mid, lo = _split3(jnp.where(lane < FOX_HEADS, c * LOG2E, 0.0))
    cpack = (hi.astype(F32) + pltpu.roll(mid.astype(F32), FOX_HEADS, 1)
             + pltpu.roll(lo.astype(F32), 2 * FOX_HEADS, 1)).astype(BF16)
    aug_q = _dot(cpack, pq_ref[...]) + oq_ref[...]
    aug_k = _dot(cpack, pk_ref[...]) + ok_ref[...]

    v = _dot(xn, w_ref[:, 2 * d:3 * d])
    gate_ref[...] = _dot(xn, wo_ref[...]).astype(gate_ref.dtype)

    bd = bd_ref[...]

    def head_norm(x, gain):
        ss = _dot((x * x).astype(BF16), bd)
        return x * lax.rsqrt(ss * (1.0 / FOX_HEAD_DIM) + EPS) * gain

    halves = (lane < FOX_HEAD_DIM, lane >= FOX_HEAD_DIM)
    for pair in range(FOX_PAIRS):
        sl = slice(pair * LANES, (pair + 1) * LANES)
        if pair % 2 == 0:
            sl2 = slice(pair * LANES, (pair + 2) * LANES)
            qn2 = head_norm(q[:, sl2], gq_ref[:, sl2]) * (FOX_HEAD_DIM ** -0.5 * LOG2E)
            kn2 = head_norm(k[:, sl2], gk_ref[:, sl2])
        inner = slice((pair % 2) * LANES, (pair % 2 + 1) * LANES)
        qn, kn = qn2[:, inner], kn2[:, inner]
        for e in range(2):
            hs = slice((2 * pair + e) * LANES, (2 * pair + e + 1) * LANES)
            qt_ref[0, hs, :] = jnp.where(halves[e], qn, aug_q[:, sl]).T.astype(BF16)
            kop_ref[:, hs] = jnp.where(halves[e], kn, aug_k[:, sl]).astype(BF16)
        vt = v[:, sl].T.astype(BF16)
        ones = jnp.ones((FOX_ONES_ROWS, vt.shape[1]), BF16)
        for e in range(2):
            r0 = (2 * pair + e) * FOX_VT_ROWS
            vt_ref[0, r0:r0 + FOX_HEAD_DIM, :] = vt[e * FOX_HEAD_DIM:(e + 1) * FOX_HEAD_DIM]
            vt_ref[0, r0 + FOX_HEAD_DIM:r0 + FOX_VT_ROWS, :] = ones


def _fox_proj_prep(x, g, w, w_o, w_f, b_f, g_q, g_k, batch, seq, ts):
    t, d = x.shape
    nblk = seq // ts
    n_op = FOX_HEADS * LANES
    bd = jnp.asarray(np.kron(np.eye(2 * LANES // FOX_HEAD_DIM), np.ones((FOX_HEAD_DIM, FOX_HEAD_DIM))), BF16)
    tri = jnp.asarray(np.tril(np.ones((ts, ts))), BF16)
    pq, pk, oq, ok = _fox_aug_matrices()
    bf = jnp.zeros((1, LANES), F32).at[0, :FOX_HEADS].set(b_f)
    gq = jnp.tile(g_q, FOX_HEADS).reshape(1, d)
    gk = jnp.tile(g_k, FOX_HEADS).reshape(1, d)
    rows = lambda b, s: (b * nblk + s, 0)
    const = lambda b, s: (0, 0)
    resident = lambda shape: pl.BlockSpec(shape, const, pipeline_mode=pl.Buffered(1))
    return pl.pallas_call(
        _fox_proj_prep_kernel,
        name="fox_proj_prep",
        grid=(batch, nblk),
        in_specs=[
            pl.BlockSpec((ts, d), rows),
            resident((1, d)),
            resident((d, 3 * d)),
            resident((d, d)),
            resident((d, LANES)),
            resident((1, LANES)),
            resident((1, d)),
            resident((1, d)),
            resident((2 * LANES, 2 * LANES)),
            resident((ts, ts)),
            resident((LANES, FOX_PAIRS * LANES)),
            resident((LANES, FOX_PAIRS * LANES)),
            resident((1, FOX_PAIRS * LANES)),
            resident((1, FOX_PAIRS * LANES)),
        ],
        out_specs=[
            pl.BlockSpec((1, n_op, ts), lambda b, s: (b, 0, s)),
            pl.BlockSpec((ts, n_op), rows),
            pl.BlockSpec((1, FOX_HEADS * FOX_VT_ROWS, ts), lambda b, s: (b, 0, s)),
            pl.BlockSpec((ts, d), rows),
        ],
        out_shape=[
            jax.ShapeDtypeStruct((batch, n_op, seq), BF16),
            jax.ShapeDtypeStruct((t, n_op), BF16),
            jax.ShapeDtypeStruct((batch, FOX_HEADS * FOX_VT_ROWS, seq), BF16),
            jax.ShapeDtypeStruct((t, d), BF16),
        ],
        scratch_shapes=[pltpu.VMEM((1, LANES), F32)],
        compiler_params=_cparams(("parallel", "arbitrary")),
    )(x, g.reshape(1, d), w, w_o, w_f, bf, gq, gk, bd, tri,
      jnp.asarray(pq, BF16), jnp.asarray(pk, BF16), jnp.asarray(oq), jnp.asarray(ok))


def _fox_attn_kernel(q_ref, k_ref, vt_ref, gate_ref, o_ref, acc_ref, m_ref, mb_ref, s_ref, *, tq):
    half = FOX_HEAD_DIM
    nq = q_ref.shape[2] // tq
    hand = (2, 3)

    def q_operands(qi):
        cols = pl.ds(pl.multiple_of(qi * tq, tq), tq)
        return [q_ref[0, e * LANES:(e + 1) * LANES, cols] for e in range(2)]

    hq = tq // 2

    def put_scores(qq, kj, slot, e, diagonal):
        start = pl.multiple_of(kj * tq, tq)
        if diagonal:
            s_ref[slot, e, 0:hq, :] = _dot(k_ref[pl.ds(start, hq), e * LANES:(e + 1) * LANES], qq[e])
            s_ref[slot, e, hq:tq, hq:tq] = _dot(k_ref[pl.ds(start + hq, hq), e * LANES:(e + 1) * LANES],
                                                qq[e][:, hq:tq])
        else:
            s_ref[slot, e] = _dot(k_ref[pl.ds(start, tq), e * LANES:(e + 1) * LANES], qq[e])

    def block(kj, slot, e, diagonal):
        start = pl.multiple_of(kj * tq, tq)
        v_rows = slice(e * FOX_VT_ROWS, (e + 1) * FOX_VT_ROWS)
        m_prev = m_ref[e]
        if diagonal:
            s_a = s_ref[slot, e, 0:hq, :]
            s_b = s_ref[slot, e, hq:tq, hq:tq]
            key = lax.broadcasted_iota(jnp.int32, (hq, tq), 0)
            qry = lax.broadcasted_iota(jnp.int32, (hq, tq), 1)
            s_a = jnp.where(key <= qry, s_a, -jnp.inf)
            key_b = lax.broadcasted_iota(jnp.int32, (hq, hq), 0)
            qry_b = lax.broadcasted_iota(jnp.int32, (hq, hq), 1)
            s_b = jnp.where(key_b <= qry_b, s_b, -jnp.inf)
            mb_ref[:, 0:hq] = jnp.full((1, hq), -jnp.inf, F32)
            mb_ref[:, hq:tq] = jnp.max(s_b, axis=0, keepdims=True)
            m_new = jnp.maximum(m_prev, jnp.maximum(jnp.max(s_a, axis=0, keepdims=True), mb_ref[...]))
            m_ref[e] = m_new
            p_a = jnp.exp2(s_a - m_new).astype(BF16)
            p_b = jnp.exp2(s_b - m_ref[e, :, hq:tq]).astype(BF16)
            acc_ref[e] = (acc_ref[e] * jnp.exp2(m_prev - m_new)
                          + _dot(vt_ref[0, v_rows, pl.ds(start, hq)], p_a))
            acc_ref[e, :, hq:tq] += _dot(vt_ref[0, v_rows, pl.ds(start + hq, hq)], p_b)
        else:
            s = s_ref[slot, e]
            m_new = jnp.maximum(m_prev, jnp.max(s, axis=0, keepdims=True))
            pv = _dot(vt_ref[0, v_rows, pl.ds(start, tq)], jnp.exp2(s - m_new).astype(BF16))
            acc_ref[e] = acc_ref[e] * jnp.exp2(m_prev - m_new) + pv
            m_ref[e] = m_new

    def step(nxt, cur, nxt_diagonal=False, cur_diagonal=False):
        for e in range(2):
            put_scores(nxt[0], nxt[1], nxt[2], e, nxt_diagonal)
            block(cur[0], cur[1], e, cur_diagonal)

    def q_block(qi, parity):
        q_rows = pl.ds(pl.multiple_of(qi * tq, tq), tq)
        qq = q_operands(qi)
        q_next = jnp.minimum(qi + 1, nq - 1)
        following = (q_operands(q_next), q_next, hand[1 - parity])

        acc_ref[...] = jnp.zeros_like(acc_ref)
        m_ref[...] = jnp.full(m_ref.shape, NEG_BIG, F32)

        def middle(j0, count):
            for u in range(count):
                step((qq, j0 + u + 1, (u + 1) % 2), (j0 + u, u % 2))

        def general():
            step((qq, 0, 0), (qi, hand[parity]), cur_diagonal=True)
            n_mid = qi - 1

            def unrolled_body(i, carry):
                middle(ATTN_UNROLL * i, ATTN_UNROLL)
                return carry

            lax.fori_loop(0, n_mid // ATTN_UNROLL, unrolled_body, 0)
            j0 = (n_mid // ATTN_UNROLL) * ATTN_UNROLL
            for rem in range(1 - parity, ATTN_UNROLL, 2):
                if rem:
                    pl.when(n_mid % ATTN_UNROLL == rem)(functools.partial(middle, j0, rem))
            step(following, (qi - 1, 1 - parity), nxt_diagonal=True)

        if parity == 0:
            pl.when(qi == 0)(lambda: step(following, (0, hand[0]), nxt_diagonal=True, cur_diagonal=True))
            pl.when(qi > 0)(general)
        else:
            general()

        outs = []
        for e in range(2):
            acc = acc_ref[e]
            denom = acc[half:half + 1]
            outs.append(acc[:half] * (1.0 / denom))
        o2 = jnp.concatenate(outs, axis=0).T
        o_ref[q_rows, :] = (o2 * _sigmoid(gate_ref[q_rows, :].astype(F32))).astype(o_ref.dtype)

    first = q_operands(0)
    for e in range(2):
        put_scores(first, 0, hand[0], e, True)

    def pair_body(jp, carry):
        q_block(2 * jp, 0)
        q_block(2 * jp + 1, 1)
        return carry

    lax.fori_loop(0, nq // 2, pair_body, 0)


def _fox_attn(qt, kop, vt, gate, batch, seq, tq):
    t = batch * seq
    return pl.pallas_call(
        functools.partial(_fox_attn_kernel, tq=tq),
        name="fox_attn",
        grid=(batch, FOX_PAIRS),
        in_specs=[
            pl.BlockSpec((1, 2 * LANES, seq), lambda b, hp: (b, hp, 0)),
            pl.BlockSpec((seq, 2 * LANES), lambda b, hp: (b, hp)),
            pl.BlockSpec((1, 2 * FOX_VT_ROWS, seq), lambda b, hp: (b, hp, 0)),
            pl.BlockSpec((seq, LANES), lambda b, hp: (b, hp)),
        ],
        out_specs=pl.BlockSpec((seq, LANES), lambda b, hp: (b, hp)),
        out_shape=jax.ShapeDtypeStruct((t, D_MODEL), BF16),
        scratch_shapes=[pltpu.VMEM((2, FOX_VT_ROWS, tq), F32), pltpu.VMEM((2, 1, tq), F32),
                        pltpu.VMEM((1, tq), F32),
                        pltpu.VMEM((4, 2, tq, tq), F32)],
        compiler_params=_cparams(("parallel", "parallel")),
    )(qt, kop, vt, gate)


def kernel(x, l0_norm_mix, l0_w_in, l0_w_gk_up, l0_b_gk, l0_g_onorm, l0_w_out, l0_norm_ffn, l0_w_ff1, l0_w_ff2, l1_norm_mix, l1_w_in, l1_b_f, l1_g_q, l1_g_k, l1_w_out, l1_norm_ffn, l1_w_ff1, l1_w_ff2, final_norm):
    batch, seq, d = x.shape
    t = batch * seq
    xf = x.reshape(t, d)

    n_main = 2 * GLA_DK + 2 * GLA_DV
    w_main = l0_w_in[:, :n_main].astype(BF16)
    w_gate = jnp.zeros((d, LANES), F32).at[:, :GLA_GATE_RANK].set(l0_w_in[:, n_main:]).astype(BF16)
    w_up = jnp.zeros((LANES, GLA_DK), F32).at[:GLA_GATE_RANK].set(l0_w_gk_up)
    og = _gla_proj(xf, l0_norm_mix, w_main, w_gate, w_up, l0_b_gk, l0_g_onorm, batch, seq, ROW_TILE)
    xf = _mix_out_mlp(og, l0_w_out.astype(BF16), xf, l0_norm_ffn, l0_w_ff1.astype(BF16),
                      l0_w_ff2.astype(BF16), final_norm, ROW_TILE, FF_CHUNK, False)

    w_qkv = l1_w_in[:, :3 * d].astype(BF16)
    w_f = l1_w_in[:, 3 * d:3 * d + FOX_HEADS]
    w_g = l1_w_in[:, 3 * d + FOX_HEADS:].astype(BF16)
    w_fp = jnp.zeros((d, LANES), F32).at[:, :FOX_HEADS].set(w_f).astype(BF16)
    qt, kop, vt, gate = _fox_proj_prep(xf, l1_norm_mix, w_qkv, w_g, w_fp, l1_b_f, l1_g_q, l1_g_k, batch, seq, ROW_TILE)
    oa = _fox_attn(qt, kop, vt, gate, batch, seq, ATTN_BLOCK)
    xf = _mix_out_mlp(oa, l1_w_out.astype(BF16), xf, l1_norm_ffn, l1_w_ff1.astype(BF16),
                      l1_w_ff2.astype(BF16), final_norm, ROW_TILE, FF_CHUNK, True)
    return xf.reshape(batch, seq, d)
```
